```python
import math
import jax, jax.numpy as jnp
from jax import lax
import numpy as np

D_MODEL = 2048
BATCH = 4
SEQ = 2048
DEPTH = 4
DEC_BATCH = 128
DEC_SEQ = 1
PAST_LEN = 16384
PAGE_SIZE = 128

N_MIXERS = 3
N_LRU_LAYERS = (DEPTH + 2) // 3
N_POOL_LAYERS = (DEPTH + 1) // 3
N_S5_LAYERS = DEPTH // 3
RMS_EPS = 1e-6
LRU_WIDTH = D_MODEL
LRU_BLOCKS = 8
LRU_BLOCK = LRU_WIDTH // LRU_BLOCKS
CONV_WIDTH = 4
LRU_C = 8.0
POOL_WINDOWS = (2, 4, 8, 16)
POOL_GROUPS = len(POOL_WINDOWS)
POOL_GROUP_DIM = D_MODEL // POOL_GROUPS
POOL_BUF = max(POOL_WINDOWS) - 1
S5_GROUP_DIM = 16
S5_GROUPS = D_MODEL // S5_GROUP_DIM
S5_STATE = 64
MEM_TOKENS = 256
XATTN_HEADS = 4
XATTN_HEAD_DIM = D_MODEL // XATTN_HEADS
D_FF = 4 * D_MODEL

kernel_name = 'hybrid_rglru_pool_s5_memxattn_step'


def rmsnorm(x, g):
    xf = x.astype(jnp.float32)
    inv = lax.rsqrt(jnp.mean(xf * xf, axis=-1, keepdims=True) + RMS_EPS)
    return (xf * inv * g.astype(jnp.float32)).astype(x.dtype)


def rg_lru_mixer(xn, conv_state, h_state, w_in, conv_w, conv_b, w_a, b_a, w_i, b_i, lam, w_o):
    f32 = jnp.float32
    B, S, _ = xn.shape
    proj = xn @ w_in
    xb, gate = proj[..., :LRU_WIDTH], proj[..., LRU_WIDTH:]
    xpad = jnp.concatenate([conv_state.astype(xb.dtype), xb], axis=1)
    xc = conv_b
    for k in range(CONV_WIDTH):
        xc = xc + conv_w[k] * xpad[:, k:k + S]
    xcf = xc.astype(f32)
    xblk = xcf.reshape(B, S, LRU_BLOCKS, LRU_BLOCK)
    r = jax.nn.sigmoid(jnp.einsum('bsnc,ncd->bsnd', xblk, w_a.astype(f32)).reshape(B, S, LRU_WIDTH) + b_a)
    i_g = jax.nn.sigmoid(jnp.einsum('bsnc,ncd->bsnd', xblk, w_i.astype(f32)).reshape(B, S, LRU_WIDTH) + b_i)
    log_a = -LRU_C * r * jax.nn.softplus(-lam.astype(f32))
    a = jnp.exp(log_a)
    u = jnp.sqrt(-jnp.expm1(2.0 * log_a)) * (i_g * xcf)

    def step(h, au):
        a_t, u_t = au
        h = a_t * h + u_t
        return h, h

    h_last, hs = lax.scan(step, h_state.astype(f32), (a.transpose(1, 0, 2), u.transpose(1, 0, 2)))
    hs = hs.transpose(1, 0, 2).astype(xn.dtype)
    out = (hs * jax.nn.gelu(gate)) @ w_o
    return out, xpad[:, -(CONV_WIDTH - 1):], h_last


def pool_mixer(xn, pool_state, pos0, w_grp, b_grp, scale):
    f32 = jnp.float32
    B, S, D = xn.shape
    xpad = jnp.concatenate([pool_state.astype(xn.dtype), xn], axis=1)
    cs = jnp.cumsum(xpad.astype(f32), axis=1)
    cs0 = jnp.concatenate([jnp.zeros((B, 1, D), f32), cs], axis=1)
    end = cs0[:, POOL_BUF + 1:]
    pos = pos0 + jnp.arange(S, dtype=jnp.int32)
    groups = []
    for g, w in enumerate(POOL_WINDOWS):
        lo, hi = g * POOL_GROUP_DIM, (g + 1) * POOL_GROUP_DIM
        start = cs0[:, POOL_BUF + 1 - w:POOL_BUF + 1 - w + S, lo:hi]
        cnt = jnp.minimum(pos + 1, w).astype(f32)[None, :, None]
        groups.append((end[..., lo:hi] - start) / cnt)
    pooled = jnp.stack(groups, axis=2)
    xg = xn.astype(f32).reshape(B, S, POOL_GROUPS, POOL_GROUP_DIM)
    mixed = jnp.einsum('bsgc,gcd->bsgd', pooled - xg, w_grp.astype(f32)).reshape(B, S, D) + b_grp
    out = (mixed * scale).astype(xn.dtype)
    return out, xpad[:, -POOL_BUF:]


def _complex_linear_combine(e1, e2):
    a1r, a1i, b1r, b1i = e1
    a2r, a2i, b2r, b2i = e2
    return (a1r * a2r - a1i * a2i, a1r * a2i + a1i * a2r,
            a2r * b1r - a2i * b1i + b2r, a2r * b1i + a2i * b1r + b2i)


def s5_mixer(xn, h_re0, h_im0, a_re, a_im, b_re, b_im, c_re, c_im, d, log_dt, w_glu, b_glu):
    f32 = jnp.float32
    B, S, D = xn.shape
    G, P = S5_GROUPS, S5_STATE
    u = xn.astype(f32).reshape(B, S, G, S5_GROUP_DIM)
    a_re, a_im = a_re.astype(f32), a_im.astype(f32)
    dt = jnp.exp(log_dt.astype(f32))[:, None]
    mag = jnp.exp(a_re * dt)
    lam_re, lam_im = mag * jnp.cos(a_im * dt), mag * jnp.sin(a_im * dt)
    den = a_re * a_re + a_im * a_im
    q_re = ((lam_re - 1.0) * a_re + lam_im * a_im) / den
    q_im = (lam_im * a_re - (lam_re - 1.0) * a_im) / den
    b_re, b_im = b_re.astype(f32), b_im.astype(f32)
    bb_re = q_re[..., None] * b_re - q_im[..., None] * b_im
    bb_im = q_re[..., None] * b_im + q_im[..., None] * b_re
    bu_re = jnp.einsum('gpc,bsgc->sbgp', bb_re, u)
    bu_im = jnp.einsum('gpc,bsgc->sbgp', bb_im, u)
    h0r, h0i = h_re0.astype(f32), h_im0.astype(f32)
    bu_re = bu_re.at[0].add(lam_re * h0r - lam_im * h0i)
    bu_im = bu_im.at[0].add(lam_re * h0i + lam_im * h0r)
    lr = jnp.broadcast_to(lam_re, (S, 1, G, P))
    li = jnp.broadcast_to(lam_im, (S, 1, G, P))
    _, _, h_re, h_im = lax.associative_scan(_complex_linear_combine, (lr, li, bu_re, bu_im), axis=0)
    y = (jnp.einsum('gcp,sbgp->bsgc', c_re.astype(f32), h_re)
         - jnp.einsum('gcp,sbgp->bsgc', c_im.astype(f32), h_im)).reshape(B, S, D)
    y = y + d * xn.astype(f32)
    z = jax.nn.gelu(y).astype(xn.dtype)
    zg = z @ w_glu + b_glu
    out = zg[..., :D] * jax.nn.sigmoid(zg[..., D:])
    return out.astype(xn.dtype), h_re[-1], h_im[-1]


def mem_kv(mem, g, w_k, w_v):
    B, M, _ = mem.shape
    mn = rmsnorm(mem, g)
    k = (mn @ w_k).reshape(B, M, XATTN_HEADS, XATTN_HEAD_DIM)
    v = (mn @ w_v).reshape(B, M, XATTN_HEADS, XATTN_HEAD_DIM)
    return k, v


def mem_xattn(xn, k, v, w_q, w_o):
    f32 = jnp.float32
    B, S, D = xn.shape
    q = (xn @ w_q).reshape(B, S, XATTN_HEADS, XATTN_HEAD_DIM)
    s = jnp.einsum('bshd,bmhd->bhsm', q.astype(f32), k.astype(f32)) * (XATTN_HEAD_DIM ** -0.5)
    p = jax.nn.softmax(s, axis=-1)
    o = jnp.einsum('bhsm,bmhd->bshd', p, v.astype(f32)).reshape(B, S, D).astype(xn.dtype)
    return o @ w_o


def sqrelu_mlp(xn, w_up, w_down):
    h = jax.nn.relu(xn @ w_up)
    return (h * h) @ w_down


def setup_inputs(seed: int = 0) -> dict:
    key = jax.random.key(seed)
    keys = iter(jax.random.split(key, 64))
    f32 = jnp.float32

    def nrm(shape, scale=1.0):
        return scale * jax.random.normal(next(keys), shape, f32)

    def gain(shape):
        return 1.0 + 0.02 * jax.random.normal(next(keys), shape, f32)

    def unif(shape, lo, hi):
        return jax.random.uniform(next(keys), shape, f32, lo, hi)

    D, R = D_MODEL, LRU_WIDTH
    NL, NP, NS = N_LRU_LAYERS, N_POOL_LAYERS, N_S5_LAYERS
    G, P, C = S5_GROUPS, S5_STATE, S5_GROUP_DIM
    kv_shape = (DEPTH, DEC_BATCH, MEM_TOKENS, XATTN_HEADS, XATTN_HEAD_DIM)
    a0 = unif((NL, R), 0.9, 0.999)
    return {
        'x_prompt': nrm((BATCH, SEQ, D)),
        'x_sample': nrm((DEC_BATCH, DEC_SEQ, D)),
        'cache_mem_k': nrm(kv_shape),
        'cache_mem_v': nrm(kv_shape),
        'state_lru_conv': nrm((NL, DEC_BATCH, CONV_WIDTH - 1, R)),
        'state_lru_h': nrm((NL, DEC_BATCH, R), 0.5),
        'state_pool': nrm((NP, DEC_BATCH, POOL_BUF, D)),
        'state_s5_re': nrm((NS, DEC_BATCH, G, P), 0.5),
        'state_s5_im': nrm((NS, DEC_BATCH, G, P), 0.5),
        'mem_prompt': nrm((BATCH, MEM_TOKENS, D)),
        'g_mix': gain((DEPTH, D)),
        'g_xattn': gain((DEPTH, D)),
        'g_mem': gain((DEPTH, D)),
        'g_mlp': gain((DEPTH, D)),
        'g_final': gain((D,)),
        'w_q': nrm((DEPTH, D, D), D ** -0.5),
        'w_k': nrm((DEPTH, D, D), D ** -0.5),
        'w_v': nrm((DEPTH, D, D), D ** -0.5),
        'w_o': nrm((DEPTH, D, D), D ** -0.5),
        'w_up': nrm((DEPTH, D, D_FF), D ** -0.5),
        'w_down': nrm((DEPTH, D_FF, D), D_FF ** -0.5),
        'lru_w_in': nrm((NL, D, 2 * R), D ** -0.5),
        'lru_conv_w': nrm((NL, CONV_WIDTH, R), CONV_WIDTH ** -0.5),
        'lru_conv_b': nrm((NL, R), 0.01),
        'lru_w_a': nrm((NL, LRU_BLOCKS, LRU_BLOCK, LRU_BLOCK), LRU_BLOCK ** -0.5),
        'lru_b_a': nrm((NL, R), 0.01),
        'lru_w_i': nrm((NL, LRU_BLOCKS, LRU_BLOCK, LRU_BLOCK), LRU_BLOCK ** -0.5),
        'lru_b_i': nrm((NL, R), 0.01),
        'lru_lambda': jnp.log(a0) - jnp.log1p(-a0),
        'lru_w_o': nrm((NL, R, D), R ** -0.5),
        'pool_w': nrm((NP, POOL_GROUPS, POOL_GROUP_DIM, POOL_GROUP_DIM), POOL_GROUP_DIM ** -0.5),
        'pool_b': nrm((NP, D), 0.01),
        'pool_scale': gain((NP, D)),
        's5_a_re': -0.5 + nrm((NS, G, P), 0.01),
        's5_a_im': math.pi * jnp.arange(P, dtype=f32) + nrm((NS, G, P), 0.01),
        's5_b_re': nrm((NS, G, P, C), (2 * C) ** -0.5),
        's5_b_im': nrm((NS, G, P, C), (2 * C) ** -0.5),
        's5_c_re': nrm((NS, G, C, P), (2 * P) ** -0.5),
        's5_c_im': nrm((NS, G, C, P), (2 * P) ** -0.5),
        's5_d': nrm((NS, D), 0.5),
        's5_log_dt': unif((NS, G), math.log(1e-3), math.log(1e-1)),
        's5_w_glu': nrm((NS, D, 2 * D), D ** -0.5),
        's5_b_glu': nrm((NS, 2 * D), 0.01),
    }


def reference(x_prompt, x_sample, cache_mem_k, cache_mem_v, state_lru_conv, state_lru_h, state_pool,
              state_s5_re, state_s5_im, mem_prompt,
              g_mix, g_xattn, g_mem, g_mlp, g_final,
              w_q, w_k, w_v, w_o, w_up, w_down,
              lru_w_in, lru_conv_w, lru_conv_b, lru_w_a, lru_b_a, lru_w_i, lru_b_i, lru_lambda, lru_w_o,
              pool_w, pool_b, pool_scale,
              s5_a_re, s5_a_im, s5_b_re, s5_b_im, s5_c_re, s5_c_im, s5_d, s5_log_dt, s5_w_glu, s5_b_glu):
    f32 = jnp.float32
    xp, xs = x_prompt, x_sample
    Bp = xp.shape[0]
    mem_k_new, mem_v_new = [], []
    lru_conv_p, lru_h_p, lru_conv_s, lru_h_s = [], [], [], []
    pool_p, pool_s = [], []
    s5_re_p, s5_im_p, s5_re_s, s5_im_s = [], [], [], []

    for i in range(DEPTH):
        kind, j = i % N_MIXERS, i // N_MIXERS
        np_ = rmsnorm(xp, g_mix[i])
        ns_ = rmsnorm(xs, g_mix[i])
        if kind == 0:
            prm = (lru_w_in[j], lru_conv_w[j], lru_conv_b[j], lru_w_a[j], lru_b_a[j],
                   lru_w_i[j], lru_b_i[j], lru_lambda[j], lru_w_o[j])
            yp, cp, hp = rg_lru_mixer(np_, jnp.zeros((Bp, CONV_WIDTH - 1, LRU_WIDTH), xp.dtype),
                                      jnp.zeros((Bp, LRU_WIDTH), f32), *prm)
            ys, cs_, hs_ = rg_lru_mixer(ns_, state_lru_conv[j], state_lru_h[j], *prm)
            lru_conv_p.append(cp); lru_h_p.append(hp)
            lru_conv_s.append(cs_); lru_h_s.append(hs_)
        elif kind == 1:
            prm = (pool_w[j], pool_b[j], pool_scale[j])
            yp, bp = pool_mixer(np_, jnp.zeros((Bp, POOL_BUF, D_MODEL), xp.dtype), 0, *prm)
            ys, bs_ = pool_mixer(ns_, state_pool[j], PAST_LEN, *prm)
            pool_p.append(bp); pool_s.append(bs_)
        else:
            prm = (s5_a_re[j], s5_a_im[j], s5_b_re[j], s5_b_im[j], s5_c_re[j], s5_c_im[j],
                   s5_d[j], s5_log_dt[j], s5_w_glu[j], s5_b_glu[j])
            z0 = jnp.zeros((Bp, S5_GROUPS, S5_STATE), f32)
            yp, rp, ip = s5_mixer(np_, z0, z0, *prm)
            ys, rs, is_ = s5_mixer(ns_, state_s5_re[j], state_s5_im[j], *prm)
            s5_re_p.append(rp); s5_im_p.append(ip)
            s5_re_s.append(rs); s5_im_s.append(is_)
        xp = xp + yp
        xs = xs + ys
        kp, vp = mem_kv(mem_prompt, g_mem[i], w_k[i], w_v[i])
        mem_k_new.append(kp); mem_v_new.append(vp)
        xp = xp + mem_xattn(rmsnorm(xp, g_xattn[i]), kp, vp, w_q[i], w_o[i])
        xs = xs + mem_xattn(rmsnorm(xs, g_xattn[i]), cache_mem_k[i], cache_mem_v[i], w_q[i], w_o[i])
        xp = xp + sqrelu_mlp(rmsnorm(xp, g_mlp[i]), w_up[i], w_down[i])
        xs = xs + sqrelu_mlp(rmsnorm(xs, g_mlp[i]), w_up[i], w_down[i])

    y_prompt = rmsnorm(xp, g_final)
    y_sample = rmsnorm(xs, g_final)
    return (y_prompt, y_sample,
            jnp.stack(mem_k_new), jnp.stack(mem_v_new),
            jnp.stack(lru_conv_p), jnp.stack(lru_h_p), jnp.stack(pool_p),
            jnp.stack(s5_re_p), jnp.stack(s5_im_p),
            jnp.stack(lru_conv_s), jnp.stack(lru_h_s), jnp.stack(pool_s),
            jnp.stack(s5_re_s), jnp.stack(s5_im_s))
```

```python
import functools
import math

import jax
import jax.numpy as jnp
from jax import lax
from jax.experimental import pallas as pl
from jax.experimental.pallas import tpu as pltpu

F32 = jnp.float32
BF16 = jnp.bfloat16

SUBLANES = 8
LANES = 128
VMEM_LIMIT_BYTES = 56 * 1024 * 1024

RMS_EPS = 1e-6
LRU_C = 8.0
POOL_WINDOWS = (2, 4, 8, 16)
POOL_BUF = max(POOL_WINDOWS) - 1
S5_GROUP_DIM = 16
S5_CHUNK = 8
S5_GROUPS_PER_BLOCK = LANES // S5_GROUP_DIM


def _params(n_axes):
    return pltpu.CompilerParams(dimension_semantics=("arbitrary",) * n_axes,
                                vmem_limit_bytes=VMEM_LIMIT_BYTES)


def _rms(x, g):
    xf = x.astype(F32)
    inv = lax.rsqrt(jnp.mean(xf * xf, axis=-1, keepdims=True) + RMS_EPS)
    return xf * inv * g.astype(F32)


def _rmsnorm_kernel(x_ref, g_ref, o_ref):
    o_ref[...] = _rms(x_ref[...], g_ref[...]).astype(o_ref.dtype)


def _rmsnorm(x, g, out_dtype, tm=512):
    rows, d = x.shape
    tm = min(tm, rows)
    return pl.pallas_call(
        _rmsnorm_kernel,
        grid=(rows // tm,),
        in_specs=[pl.BlockSpec((tm, d), lambda i: (i, 0)),
                  pl.BlockSpec((1, d), lambda i: (0, 0))],
        out_specs=pl.BlockSpec((tm, d), lambda i: (i, 0)),
        out_shape=jax.ShapeDtypeStruct((rows, d), out_dtype),
        compiler_params=_params(1),
        name="rmsnorm",
    )(x, g.reshape(1, d))


def _mm_kernel(*refs, n_w, n_row, n_full, epilogue):
    x_ref = refs[0]
    w_refs = refs[1:1 + n_w]
    row_refs = refs[1 + n_w:1 + n_w + n_row]
    full_refs = refs[1 + n_w + n_row:1 + n_w + n_row + n_full]
    o_ref = refs[1 + n_w + n_row + n_full]
    wbf_refs = refs[2 + n_w + n_row + n_full:]

    @pl.when(pl.program_id(1) == 0)
    def _():
        for w_ref, wbf_ref in zip(w_refs, wbf_refs):
            wbf_ref[...] = w_ref[...].astype(BF16)

    xb = x_ref[...].astype(BF16)
    accs = [jnp.dot(xb, wbf_ref[...], preferred_element_type=F32) for wbf_ref in wbf_refs]
    out = epilogue(accs, [r[...] for r in row_refs], [f[...] for f in full_refs])
    o_ref[...] = out.astype(o_ref.dtype)


def _mm(x, ws, n_cols, epilogue, out_dtype, *, rows_extra=(), full_extra=(), tm=1024, tn=512,
        name="mm"):
    rows, k = x.shape
    tm = min(tm, rows)
    tn = min(tn, n_cols)
    in_specs = [pl.BlockSpec((tm, k), lambda j, i: (i, 0))]
    args = [x]
    for w, off in ws:
        in_specs.append(pl.BlockSpec((k, tn), lambda j, i, o=off // tn: (0, o + j)))
        args.append(w)
    for v, off in rows_extra:
        in_specs.append(pl.BlockSpec((1, tn), lambda j, i, o=off // tn: (0, o + j)))
        args.append(v)
    for f in full_extra:
        in_specs.append(pl.BlockSpec((tm, tn), lambda j, i: (i, j)))
        args.append(f)
    kern = functools.partial(_mm_kernel, n_w=len(ws), n_row=len(rows_extra),
                             n_full=len(full_extra), epilogue=epilogue)
    return pl.pallas_call(
        kern,
        grid=(n_cols // tn, rows // tm),
        in_specs=in_specs,
        out_specs=pl.BlockSpec((tm, tn), lambda j, i: (i, j)),
        out_shape=jax.ShapeDtypeStruct((rows, n_cols), out_dtype),
        scratch_shapes=[pltpu.VMEM((k, tn), BF16) for _ in ws],
        compiler_params=_params(2),
        name=name,
    )(*args)


def _ep_plain(accs, rows, fulls):
    return accs[0]


def _ep_residual(accs, rows, fulls):
    return fulls[0] + accs[0]


def _ep_glu_residual(accs, rows, fulls):
    a = accs[0] + rows[0]
    b = accs[1] + rows[1]
    return fulls[0] + a * jax.nn.sigmoid(b)


def _mlp_kernel(x_ref, g_ref, wu_ref, wd_ref, o_ref, xn_ref):
    @pl.when(pl.program_id(1) == 0)
    def _():
        xv = x_ref[...]
        xn_ref[...] = _rms(xv, g_ref[...]).astype(BF16)
        o_ref[...] = xv

    h = jnp.dot(xn_ref[...], wu_ref[...], preferred_element_type=F32)
    h = jnp.maximum(h, 0.0)
    o_ref[...] += jnp.dot((h * h).astype(BF16), wd_ref[...], preferred_element_type=F32)


def _mlp(x, g, wu_bf, wd_bf, tm=512, tf=512):
    rows, d = x.shape
    dff = wu_bf.shape[1]
    tm = min(tm, rows)
    return pl.pallas_call(
        _mlp_kernel,
        grid=(rows // tm, dff // tf),
        in_specs=[pl.BlockSpec((tm, d), lambda i, f: (i, 0)),
                  pl.BlockSpec((1, d), lambda i, f: (0, 0)),
                  pl.BlockSpec((d, tf), lambda i, f: (0, f)),
                  pl.BlockSpec((tf, d), lambda i, f: (f, 0))],
        out_specs=pl.BlockSpec((tm, d), lambda i, f: (i, 0)),
        out_shape=jax.ShapeDtypeStruct((rows, d), F32),
        scratch_shapes=[pltpu.VMEM((tm, d), BF16)],
        compiler_params=_params(2),
        name="mlp",
    )(x, g.reshape(1, d), wu_bf, wd_bf)


def _attn_prompt_kernel(q_ref, k_ref, v_ref, o_ref, *, scale):
    q = q_ref[...]
    k = k_ref[0].astype(BF16)
    s = lax.dot_general(q, k, (((1,), (1,)), ((), ())), preferred_element_type=F32) * scale
    m = jnp.max(s, axis=-1, keepdims=True)
    e = jnp.exp(s - m)
    p = e / jnp.sum(e, axis=-1, keepdims=True)
    o = jnp.dot(p.astype(BF16), v_ref[0].astype(BF16), preferred_element_type=F32)
    o_ref[...] = o.astype(o_ref.dtype)


def _attn_prompt(q, k, v, n_seq, seq, heads, tq=1024):
    rows, d = q.shape
    m = k.shape[1]
    hd = d // heads
    tq = min(tq, seq)
    tps = seq // tq
    kern = functools.partial(_attn_prompt_kernel, scale=hd ** -0.5)
    return pl.pallas_call(
        kern,
        grid=(n_seq, tps, heads),
        in_specs=[pl.BlockSpec((tq, hd), lambda b, i, h: (b * tps + i, h)),
                  pl.BlockSpec((1, m, hd), lambda b, i, h: (b, 0, h)),
                  pl.BlockSpec((1, m, hd), lambda b, i, h: (b, 0, h))],
        out_specs=pl.BlockSpec((tq, hd), lambda b, i, h: (b * tps + i, h)),
        out_shape=jax.ShapeDtypeStruct((rows, d), BF16),
        compiler_params=_params(3),
        name="attn_prompt",
    )(q, k, v)


def _attn_sample_kernel(q_ref, k_ref, v_ref, o_ref, *, heads, scale, bt):
    d = q_ref.shape[-1]
    hd = d // heads
    for b in range(bt):
        q = q_ref[b]
        prod = k_ref[b] * q
        outs = []
        for h in range(heads):
            sl = slice(h * hd, (h + 1) * hd)
            s = jnp.sum(prod[:, sl], axis=-1, keepdims=True) * scale
            mx = jnp.max(s, axis=0, keepdims=True)
            e = jnp.exp(s - mx)
            p = e / jnp.sum(e, axis=0, keepdims=True)
            outs.append(jnp.sum(p * v_ref[b, :, sl], axis=0, keepdims=True))
        o_ref[b] = jnp.concatenate(outs, axis=-1).astype(o_ref.dtype)


def _attn_sample(q, k, v, heads, bt=2):
    bsz, d = q.shape
    m = k.shape[1]
    kern = functools.partial(_attn_sample_kernel, heads=heads, scale=(d // heads) ** -0.5, bt=bt)
    out = pl.pallas_call(
        kern,
        grid=(bsz // bt,),
        in_specs=[pl.BlockSpec((bt, 1, d), lambda i: (i, 0, 0)),
                  pl.BlockSpec((bt, m, d), lambda i: (i, 0, 0)),
                  pl.BlockSpec((bt, m, d), lambda i: (i, 0, 0))],
        out_specs=pl.BlockSpec((bt, 1, d), lambda i: (i, 0, 0)),
        out_shape=jax.ShapeDtypeStruct((bsz, 1, d), BF16),
        compiler_params=_params(1),
        name="attn_sample",
    )(q.reshape(bsz, 1, d), k, v)
    return out.reshape(bsz, d)


def _lru_gates(xc, wa, ba, wi, bi, lam):
    xcb = xc.astype(BF16)
    r = jax.nn.sigmoid(jnp.dot(xcb, wa.astype(BF16), preferred_element_type=F32) + ba)
    ig = jax.nn.sigmoid(jnp.dot(xcb, wi.astype(BF16), preferred_element_type=F32) + bi)
    log_a = -LRU_C * r * jax.nn.softplus(-lam)
    a = jnp.exp(log_a)
    u = jnp.sqrt(-jnp.tanh(log_a) * (a * a + 1.0)) * (ig * xc)
    return a, u


def _lru_prompt_kernel(xb_ref, gt_ref, cw_ref, cb_ref, wa_ref, ba_ref, wi_ref, bi_ref, lam_ref,
                       y_ref, cst_ref, hl_ref, ext_ref, a_ref, u_ref, hs_ref, hc_ref,
                       *, tiles_per_seq, taps):
    tm, cb = xb_ref.shape
    halo = SUBLANES

    @pl.when(pl.program_id(1) % tiles_per_seq == 0)
    def _():
        ext_ref[0:halo, :] = jnp.zeros((halo, cb), F32)
        hc_ref[...] = jnp.zeros((1, cb), F32)

    ext_ref[halo:halo + tm, :] = xb_ref[...]
    cw = cw_ref[...]
    xc = cb_ref[...]
    for k in range(taps):
        start = halo - (taps - 1) + k
        xc = xc + cw[k:k + 1, :] * ext_ref[start:start + tm, :]
    cst_ref[0] = ext_ref[halo + tm - (taps - 1):halo + tm, :]
    ext_ref[0:halo, :] = ext_ref[tm:tm + halo, :]

    a, u = _lru_gates(xc, wa_ref[0], ba_ref[...], wi_ref[0], bi_ref[...], lam_ref[...])

    row = lax.broadcasted_iota(jnp.int32, (tm, cb), 0) % SUBLANES
    s = 1
    while s < SUBLANES:
        keep = row >= s
        u = u + a * jnp.where(keep, pltpu.roll(u, s, 0), 0.0)
        a = a * jnp.where(keep, pltpu.roll(a, s, 0), 1.0)
        s *= 2
    a_ref[...] = a
    u_ref[...] = u

    def link(g, h):
        off = pl.multiple_of(g * SUBLANES, SUBLANES)
        ag = a_ref[pl.ds(off, SUBLANES), :]
        ug = u_ref[pl.ds(off, SUBLANES), :]
        hs_ref[pl.ds(off, SUBLANES), :] = ag * h + ug
        return ag[SUBLANES - 1:SUBLANES, :] * h + ug[SUBLANES - 1:SUBLANES, :]

    h_last = lax.fori_loop(0, tm // SUBLANES, link, hc_ref[...], unroll=4)
    hc_ref[...] = h_last
    hl_ref[0] = h_last
    y_ref[...] = (hs_ref[...] * jax.nn.gelu(gt_ref[...])).astype(y_ref.dtype)


def _lru_prompt(proj, conv_w, conv_b, w_a, b_a, w_i, b_i, lam, n_seq, seq, tm=1024):
    rows, r2 = proj.shape
    r = r2 // 2
    nb, cb = w_a.shape[0], w_a.shape[1]
    taps = conv_w.shape[0]
    tm = min(tm, seq)
    tps = seq // tm
    kern = functools.partial(_lru_prompt_kernel, tiles_per_seq=tps, taps=taps)
    vec = lambda c, i: (0, c)
    y, cst, hl = pl.pallas_call(
        kern,
        grid=(nb, rows // tm),
        in_specs=[pl.BlockSpec((tm, cb), lambda c, i: (i, c)),
                  pl.BlockSpec((tm, cb), lambda c, i: (i, nb + c)),
                  pl.BlockSpec((taps, cb), vec),
                  pl.BlockSpec((1, cb), vec),
                  pl.BlockSpec((1, cb, cb), lambda c, i: (c, 0, 0)),
                  pl.BlockSpec((1, cb), vec),
                  pl.BlockSpec((1, cb, cb), lambda c, i: (c, 0, 0)),
                  pl.BlockSpec((1, cb), vec),
                  pl.BlockSpec((1, cb), vec)],
        out_specs=[pl.BlockSpec((tm, cb), lambda c, i: (i, c)),
                   pl.BlockSpec((1, taps - 1, cb), lambda c, i: (i // tps, 0, c)),
                   pl.BlockSpec((1, 1, cb), lambda c, i: (i // tps, 0, c))],
        out_shape=[jax.ShapeDtypeStruct((rows, r), BF16),
                   jax.ShapeDtypeStruct((n_seq, taps - 1, r), F32),
                   jax.ShapeDtypeStruct((n_seq, 1, r), F32)],
        scratch_shapes=[pltpu.VMEM((tm + SUBLANES, cb), F32),
                        pltpu.VMEM((tm, cb), F32),
                        pltpu.VMEM((tm, cb), F32),
                        pltpu.VMEM((tm, cb), F32),
                        pltpu.VMEM((1, cb), F32)],
        compiler_params=_params(2),
        name="lru_prompt",
    )(proj, proj, conv_w, conv_b.reshape(1, r), w_a, b_a.reshape(1, r), w_i, b_i.reshape(1, r),
      lam.reshape(1, r))
    return y, cst, hl.reshape(n_seq, r)


def _lru_sample_kernel(xb_ref, gt_ref, c0_ref, c1_ref, c2_ref, h0_ref, cw_ref, cb_ref,
                       wa_ref, ba_ref, wi_ref, bi_ref, lam_ref, y_ref, h_ref):
    cw = cw_ref[...]
    xc = cb_ref[...]
    for k, c_ref in enumerate((c0_ref, c1_ref, c2_ref, xb_ref)):
        xc = xc + cw[k:k + 1, :] * c_ref[...]
    a, u = _lru_gates(xc, wa_ref[0], ba_ref[...], wi_ref[0], bi_ref[...], lam_ref[...])
    h = a * h0_ref[...] + u
    h_ref[...] = h
    y_ref[...] = (h * jax.nn.gelu(gt_ref[...])).astype(y_ref.dtype)


def _lru_sample(proj, conv_rows, h0, conv_w, conv_b, w_a, b_a, w_i, b_i, lam):
    bsz, r2 = proj.shape
    r = r2 // 2
    nb, cb = w_a.shape[0], w_a.shape[1]
    taps = conv_w.shape[0]
    blk = lambda c: (0, c)
    return pl.pallas_call(
        _lru_sample_kernel,
        grid=(nb,),
        in_specs=[pl.BlockSpec((bsz, cb), blk),
                  pl.BlockSpec((bsz, cb), lambda c: (0, nb + c)),
                  pl.BlockSpec((bsz, cb), blk), pl.BlockSpec((bsz, cb), blk),
                  pl.BlockSpec((bsz, cb), blk), pl.BlockSpec((bsz, cb), blk),
                  pl.BlockSpec((taps, cb), blk), pl.BlockSpec((1, cb), blk),
                  pl.BlockSpec((1, cb, cb), lambda c: (c, 0, 0)), pl.BlockSpec((1, cb), blk),
                  pl.BlockSpec((1, cb, cb), lambda c: (c, 0, 0)), pl.BlockSpec((1, cb), blk),
                  pl.BlockSpec((1, cb), blk)],
        out_specs=[pl.BlockSpec((bsz, cb), blk), pl.BlockSpec((bsz, cb), blk)],
        out_shape=[jax.ShapeDtypeStruct((bsz, r), BF16), jax.ShapeDtypeStruct((bsz, r), F32)],
        compiler_params=_params(1),
        name="lru_sample",
    )(proj, proj, conv_rows[0], conv_rows[1], conv_rows[2], h0, conv_w, conv_b.reshape(1, r),
      w_a, b_a.reshape(1, r), w_i, b_i.reshape(1, r), lam.reshape(1, r))


def _pool_prompt_kernel(xn_ref, x_ref, w_ref, b_ref, sc_ref, o_ref, st_ref, ext_ref, wbf_ref,
                        *, tiles_per_seq, windows):
    tm, d = xn_ref.shape
    gd = d // len(windows)
    halo = 2 * SUBLANES
    i = pl.program_id(0)

    @pl.when(i == 0)
    def _():
        wbf_ref[...] = w_ref[...].astype(BF16)

    @pl.when(i % tiles_per_seq == 0)
    def _():
        ext_ref[0:halo, :] = jnp.zeros((halo, d), F32)

    ext_ref[halo:halo + tm, :] = xn_ref[...]
    pos = (i % tiles_per_seq) * tm + lax.broadcasted_iota(jnp.int32, (tm, 1), 0)
    for g, w in enumerate(windows):
        sl = slice(g * gd, (g + 1) * gd)
        s = ext_ref[:, sl]
        shift = 1
        while shift < w:
            s = s + pltpu.roll(s, shift, 0)
            shift *= 2
        cnt = jnp.minimum(pos + 1, w).astype(F32)
        pooled = s[halo:, :] / cnt
        diff = (pooled - xn_ref[:, sl]).astype(BF16)
        mixed = jnp.dot(diff, wbf_ref[g], preferred_element_type=F32) + b_ref[:, sl]
        o_ref[:, sl] = x_ref[:, sl] + mixed * sc_ref[:, sl]
    st_ref[0] = ext_ref[halo + tm - POOL_BUF:halo + tm, :]
    ext_ref[0:halo, :] = ext_ref[tm:tm + halo, :]


def _pool_prompt(xn, x, w, b, scale, n_seq, seq, tm=512):
    rows, d = xn.shape
    tm = min(tm, seq)
    tps = seq // tm
    ng, gd = w.shape[0], w.shape[1]
    kern = functools.partial(_pool_prompt_kernel, tiles_per_seq=tps, windows=POOL_WINDOWS)
    return pl.pallas_call(
        kern,
        grid=(rows // tm,),
        in_specs=[pl.BlockSpec((tm, d), lambda i: (i, 0)),
                  pl.BlockSpec((tm, d), lambda i: (i, 0)),
                  pl.BlockSpec((ng, gd, gd), lambda i: (0, 0, 0)),
                  pl.BlockSpec((1, d), lambda i: (0, 0)),
                  pl.BlockSpec((1, d), lambda i: (0, 0))],
        out_specs=[pl.BlockSpec((tm, d), lambda i: (i, 0)),
                   pl.BlockSpec((1, POOL_BUF, d), lambda i: (i // tps, 0, 0))],
        out_shape=[jax.ShapeDtypeStruct((rows, d), F32),
                   jax.ShapeDtypeStruct((n_seq, POOL_BUF, d), F32)],
        scratch_shapes=[pltpu.VMEM((tm + 2 * SUBLANES, d), F32),
                        pltpu.VMEM((ng, gd, gd), BF16)],
        compiler_params=_params(1),
        name="pool_prompt",
    )(xn, x, w, b.reshape(1, d), scale.reshape(1, d))


def _pool_sample_kernel(xn_ref, x_ref, st_ref, w_ref, b_ref, sc_ref, o_ref, *, window, pos0):
    xn = xn_ref[...]
    s = xn
    for r in range(POOL_BUF - (window - 1), POOL_BUF):
        s = s + st_ref[r]
    pooled = s / float(min(pos0 + 1, window))
    diff = (pooled - xn).astype(BF16)
    mixed = jnp.dot(diff, w_ref[0].astype(BF16), preferred_element_type=F32) + b_ref[...]
    o_ref[...] = x_ref[...] + mixed * sc_ref[...]


def _pool_sample(xn, x, state_t, w, b, scale, pos0):
    bsz, d = xn.shape
    ng, gd = w.shape[0], w.shape[1]
    outs = []
    for g, window in enumerate(POOL_WINDOWS):
        kern = functools.partial(_pool_sample_kernel, window=window, pos0=pos0)
        blk = lambda i, g=g: (0, g)
        outs.append(pl.pallas_call(
            kern,
            grid=(1,),
            in_specs=[pl.BlockSpec((bsz, gd), blk), pl.BlockSpec((bsz, gd), blk),
                      pl.BlockSpec((POOL_BUF, bsz, gd), lambda i, g=g: (0, 0, g)),
                      pl.BlockSpec((1, gd, gd), lambda i, g=g: (g, 0, 0)),
                      pl.BlockSpec((1, gd), blk), pl.BlockSpec((1, gd), blk)],
            out_specs=pl.BlockSpec((bsz, gd), lambda i: (0, 0)),
            out_shape=jax.ShapeDtypeStruct((bsz, gd), F32),
            compiler_params=_params(1),
            name="pool_sample",
        )(xn, x, state_t, w, b.reshape(1, d), scale.reshape(1, d)))
    return jnp.concatenate(outs, axis=-1)


def _s5_matrices(a_re, a_im, b_re, b_im, c_re, c_im, log_dt, chunk):
    g_all, p = a_re.shape
    c = b_re.shape[-1]
    gpb = S5_GROUPS_PER_BLOCK
    nb = g_all // gpb
    dt = jnp.exp(log_dt.astype(F32))[:, None]
    mag = jnp.exp(a_re * dt)
    lr, li = mag * jnp.cos(a_im * dt), mag * jnp.sin(a_im * dt)
    den = a_re * a_re + a_im * a_im
    q_re = ((lr - 1.0) * a_re + li * a_im) / den
    q_im = (li * a_re - (lr - 1.0) * a_im) / den
    bb_re = q_re[..., None] * b_re - q_im[..., None] * b_im
    bb_im = q_re[..., None] * b_im + q_im[..., None] * b_re

    def cmul(xr, xi, yr, yi):
        return xr * yr - xi * yi, xr * yi + xi * yr

    pw = [(jnp.ones_like(lr), jnp.zeros_like(lr))]
    for _ in range(chunk):
        pw.append(cmul(pw[-1][0], pw[-1][1], lr, li))
    eye = jnp.eye(gpb, dtype=F32)
    hi = lax.Precision.HIGHEST

    ks = []
    for tau in range(chunk):
        wr, wi = cmul(c_re, c_im, pw[tau][0][:, None, :], pw[tau][1][:, None, :])
        ks.append(jnp.einsum('gcp,gpd->gcd', wr, bb_re, precision=hi)
                  - jnp.einsum('gcp,gpd->gcd', wi, bb_im, precision=hi))
    zero = jnp.zeros_like(ks[0])
    toep = jnp.stack([jnp.stack([ks[t - s] if t >= s else zero for t in range(chunk)])
                      for s in range(chunk)])
    toep = toep.reshape(chunk, chunk, nb, gpb, c, c).transpose(2, 0, 3, 5, 1, 4)
    m_in = toep[:, :, :, :, :, None, :] * eye[None, None, :, None, None, :, None]
    m_in = m_in.reshape(nb, chunk * LANES, chunk * LANES).astype(BF16)

    es = []
    for t in range(chunk):
        pr, pi = pw[chunk - 1 - t]
        er, ei = cmul(pr[..., None], pi[..., None], bb_re, bb_im)
        es.append(jnp.stack([er, ei]))
    es = jnp.stack(es).reshape(chunk, 2, nb, gpb, p, c).transpose(2, 0, 3, 5, 1, 4)
    m_state = es[:, :, :, :, :, None, :] * eye[None, None, :, None, None, :, None]
    m_state = m_state.reshape(nb, chunk * LANES, 2 * gpb * p).astype(BF16)

    ws = []
    for t in range(chunk):
        wr, wi = cmul(c_re, c_im, pw[t + 1][0][:, None, :], pw[t + 1][1][:, None, :])
        ws.append(jnp.stack([wr, -wi]))
    ws = jnp.stack(ws).reshape(chunk, 2, nb, gpb, c, p).transpose(2, 1, 3, 5, 0, 4)
    m_out = ws[:, :, :, :, :, None, :] * eye[None, None, :, None, None, :, None]
    m_out = m_out.reshape(nb, 2 * gpb * p, chunk * LANES).astype(BF16)

    qr, qi = pw[chunk]
    rows = [(jnp.ones_like(qr), jnp.zeros_like(qr))]
    for _ in range(SUBLANES):
        rows.append(cmul(rows[-1][0], rows[-1][1], qr, qi))
    table = jnp.stack([_s5_state_layout(r_, i_) for r_, i_ in rows])
    table = jnp.concatenate([table, jnp.zeros((2 * SUBLANES - table.shape[0], table.shape[1]), F32)])
    return m_in, m_state, m_out, table


def _s5_state_layout(re, im):
    lead = re.shape[:-2]
    g_all, p = re.shape[-2:]
    nb = g_all // S5_GROUPS_PER_BLOCK
    st = jnp.stack([re.reshape(lead + (nb, S5_GROUPS_PER_BLOCK * p)),
                    im.reshape(lead + (nb, S5_GROUPS_PER_BLOCK * p))], axis=-2)
    return st.reshape(lead + (2 * g_all * p,))


def _s5_state_unlayout(flat, g_all, p):
    lead = flat.shape[:-1]
    nb = g_all // S5_GROUPS_PER_BLOCK
    st = flat.reshape(lead + (nb, 2, S5_GROUPS_PER_BLOCK * p))
    return st[..., 0, :].reshape(lead + (g_all, p)), st[..., 1, :].reshape(lead + (g_all, p))


def _s5_lhs(u_refs):
    return jnp.concatenate([u[...] for u in u_refs], axis=1).astype(BF16)


def _s5_state_in_kernel(*refs, chunk):
    u_refs, m_ref, e_ref = refs[:chunk], refs[chunk], refs[chunk + 1]
    e_ref[...] = jnp.dot(_s5_lhs(u_refs), m_ref[0], preferred_element_type=F32)


def _s5_state_in(xnv, m_state, chunk, d, tr=256):
    r = xnv.shape[0]
    nb, _, sw = m_state.shape
    tr = min(tr, r)
    lb = d // LANES
    kern = functools.partial(_s5_state_in_kernel, chunk=chunk)
    return pl.pallas_call(
        kern,
        grid=(nb, r // tr),
        in_specs=[pl.BlockSpec((tr, LANES), lambda b, i, t=t: (i, t * lb + b)) for t in range(chunk)]
        + [pl.BlockSpec((1, chunk * LANES, sw), lambda b, i: (b, 0, 0))],
        out_specs=pl.BlockSpec((tr, sw), lambda b, i: (i, b)),
        out_shape=jax.ShapeDtypeStruct((r, nb * sw), F32),
        compiler_params=_params(2),
        name="s5_state_in",
    )(*([xnv] * chunk), m_state)


def _s5_scan_kernel(e_ref, tab_ref, h_ref, fin_ref):
    rps, sw = e_ref.shape
    half = sw // 2
    er, ei = e_ref[:, :half], e_ref[:, half:]
    row = lax.broadcasted_iota(jnp.int32, (rps, half), 0) % SUBLANES
    s = 1
    while s < SUBLANES:
        keep = row >= s
        pr, pi = tab_ref[s:s + 1, :half], tab_ref[s:s + 1, half:]
        sr = jnp.where(keep, pltpu.roll(er, s, 0), 0.0)
        si = jnp.where(keep, pltpu.roll(ei, s, 0), 0.0)
        er, ei = er + pr * sr - pi * si, ei + pr * si + pi * sr
        s *= 2
    xr = jnp.where(row >= 1, pltpu.roll(er, 1, 0), 0.0)
    xi = jnp.where(row >= 1, pltpu.roll(ei, 1, 0), 0.0)
    ar, ai = tab_ref[0:SUBLANES, :half], tab_ref[0:SUBLANES, half:]
    l8r, l8i = tab_ref[SUBLANES:SUBLANES + 1, :half], tab_ref[SUBLANES:SUBLANES + 1, half:]
    cr = jnp.zeros((1, half), F32)
    ci = jnp.zeros((1, half), F32)
    for g in range(rps // SUBLANES):
        lo, hi = g * SUBLANES, (g + 1) * SUBLANES
        h_ref[lo:hi, :half] = ar * cr - ai * ci + xr[lo:hi]
        h_ref[lo:hi, half:] = ar * ci + ai * cr + xi[lo:hi]
        cr, ci = (l8r * cr - l8i * ci + er[hi - 1:hi], l8r * ci + l8i * cr + ei[hi - 1:hi])
    fin_ref[0, :, :half] = cr
    fin_ref[0, :, half:] = ci


def _s5_scan(e, table, n_seq, sw):
    r, width = e.shape
    rps = r // n_seq
    return pl.pallas_call(
        _s5_scan_kernel,
        grid=(width // sw, n_seq),
        in_specs=[pl.BlockSpec((rps, sw), lambda c, b: (b, c)),
                  pl.BlockSpec((2 * SUBLANES, sw), lambda c, b: (0, c))],
        out_specs=[pl.BlockSpec((rps, sw), lambda c, b: (b, c)),
                   pl.BlockSpec((1, 1, sw), lambda c, b: (b, 0, c))],
        out_shape=[jax.ShapeDtypeStruct((r, width), F32),
                   jax.ShapeDtypeStruct((n_seq, 1, width), F32)],
        compiler_params=_params(2),
        name="s5_scan",
    )(e, table)


def _s5_step_kernel(e_ref, h0_ref, tab_ref, o_ref):
    half = e_ref.shape[1] // 2
    lr, li = tab_ref[1:2, :half], tab_ref[1:2, half:]
    hr, hi = h0_ref[:, :half], h0_ref[:, half:]
    o_ref[:, :half] = lr * hr - li * hi + e_ref[:, :half]
    o_ref[:, half:] = lr * hi + li * hr + e_ref[:, half:]


def _s5_step(e, h0, table, sw):
    bsz, width = e.shape
    return pl.pallas_call(
        _s5_step_kernel,
        grid=(width // sw,),
        in_specs=[pl.BlockSpec((bsz, sw), lambda c: (0, c)),
                  pl.BlockSpec((bsz, sw), lambda c: (0, c)),
                  pl.BlockSpec((2 * SUBLANES, sw), lambda c: (0, c))],
        out_specs=pl.BlockSpec((bsz, sw), lambda c: (0, c)),
        out_shape=jax.ShapeDtypeStruct((bsz, width), F32),
        compiler_params=_params(1),
        name="s5_step",
    )(e, h0, table)


def _s5_out_kernel(*refs, chunk):
    u_refs = refs[:chunk]
    h_ref, mi_ref, mo_ref = refs[chunk:chunk + 3]
    y_refs = refs[chunk + 3:]
    y = jnp.dot(_s5_lhs(u_refs), mi_ref[0], preferred_element_type=F32)
    y = y + jnp.dot(h_ref[...].astype(BF16), mo_ref[0], preferred_element_type=F32)
    for t, y_ref in enumerate(y_refs):
        y_ref[...] = y[:, t * LANES:(t + 1) * LANES]


def _s5_out(xnv, h, m_in, m_out, chunk, d, tr=256):
    r = xnv.shape[0]
    nb, sw, _ = m_out.shape
    tr = min(tr, r)
    lb = d // LANES
    kern = functools.partial(_s5_out_kernel, chunk=chunk)
    return pl.pallas_call(
        kern,
        grid=(nb, r // tr),
        in_specs=[pl.BlockSpec((tr, LANES), lambda b, i, t=t: (i, t * lb + b)) for t in range(chunk)]
        + [pl.BlockSpec((tr, sw), lambda b, i: (i, b)),
           pl.BlockSpec((1, chunk * LANES, chunk * LANES), lambda b, i: (b, 0, 0)),
           pl.BlockSpec((1, sw, chunk * LANES), lambda b, i: (b, 0, 0))],
        out_specs=[pl.BlockSpec((tr, LANES), lambda b, i: (i, b)) for _ in range(chunk)],
        out_shape=[jax.ShapeDtypeStruct((r, d), F32) for _ in range(chunk)],
        compiler_params=_params(2),
        name="s5_out",
    )(*([xnv] * chunk), h, m_in, m_out)


def _s5_gelu_kernel(y_ref, xn_ref, d_ref, z_ref):
    z_ref[...] = jax.nn.gelu(y_ref[...] + d_ref[...] * xn_ref[...]).astype(z_ref.dtype)


def _s5_gelu(y, xn, dvec, tm=512):
    rows, d = xn.shape
    tm = min(tm, rows)
    blk = pl.BlockSpec((tm, d), lambda i: (i, 0))
    return pl.pallas_call(
        _s5_gelu_kernel,
        grid=(rows // tm,),
        in_specs=[blk, blk, pl.BlockSpec((1, d), lambda i: (0, 0))],
        out_specs=blk,
        out_shape=jax.ShapeDtypeStruct((rows, d), BF16),
        compiler_params=_params(1),
        name="s5_gelu",
    )(y, xn, dvec.reshape(1, d))


def _s5_mixer(x, xn, h0_flat, n_seq, prm, chunk):
    a_re, a_im, b_re, b_im, c_re, c_im, dvec, log_dt, w_glu, b_glu = prm
    rows, d = xn.shape
    m_in, m_state, m_out, table = _s5_matrices(a_re, a_im, b_re, b_im, c_re, c_im, log_dt, chunk)
    sw = m_state.shape[-1]
    xnv = xn.reshape(rows // chunk, chunk * d)
    e = _s5_state_in(xnv, m_state, chunk, d)
    if h0_flat is None:
        h, fin = _s5_scan(e, table, n_seq, sw)
        fin = fin.reshape(n_seq, -1)
    else:
        h = h0_flat
        fin = _s5_step(e, h0_flat, table, sw)
    ys = _s5_out(xnv, h, m_in, m_out, chunk, d)
    y = jnp.stack(ys, axis=1).reshape(rows, d)
    z = _s5_gelu(y, xn, dvec)
    x_new = _mm(z, [(w_glu, 0), (w_glu, d)], d, _ep_glu_residual, F32,
                rows_extra=[(b_glu.reshape(1, 2 * d), 0), (b_glu.reshape(1, 2 * d), d)],
                full_extra=[x], name="s5_glu")
    return x_new, fin


def kernel(x_prompt, x_sample, cache_mem_k, cache_mem_v, state_lru_conv, state_lru_h, state_pool, state_s5_re, state_s5_im, mem_prompt, g_mix, g_xattn, g_mem, g_mlp, g_final, w_q, w_k, w_v, w_o, w_up, w_down, lru_w_in, lru_conv_w, lru_conv_b, lru_w_a, lru_b_a, lru_w_i, lru_b_i, lru_lambda, lru_w_o, pool_w, pool_b, pool_scale, s5_a_re, s5_a_im, s5_b_re, s5_b_im, s5_c_re, s5_c_im, s5_d, s5_log_dt, s5_w_glu, s5_b_glu):
    n_seq, seq, d = x_prompt.shape
    bsz, dec_seq, _ = x_sample.shape
    assert dec_seq == 1, "the sample group advances one token per request"
    depth = g_mix.shape[0]
    mem_tokens = mem_prompt.shape[1]
    heads, head_dim = cache_mem_k.shape[3], cache_mem_k.shape[4]
    g_all, p_state = s5_a_re.shape[1], s5_a_re.shape[2]
    past_len = 16384

    xp = x_prompt.reshape(n_seq * seq, d)
    xs = x_sample.reshape(bsz, d)
    mem = mem_prompt.reshape(n_seq * mem_tokens, d)
    wu_bf = w_up.astype(BF16)
    wd_bf = w_down.astype(BF16)

    mem_k, mem_v = [], []
    lru_conv_p, lru_h_p, lru_conv_s, lru_h_s = [], [], [], []
    pool_p, pool_s = [], []
    s5_re_p, s5_im_p, s5_re_s, s5_im_s = [], [], [], []

    for i in range(depth):
        kind, j = i % 3, i // 3
        if kind == 0:
            prm = (lru_conv_w[j], lru_conv_b[j], lru_w_a[j], lru_b_a[j], lru_w_i[j], lru_b_i[j],
                   lru_lambda[j])
            proj = _mm(_rmsnorm(xp, g_mix[i], BF16), [(lru_w_in[j], 0)], 2 * d, _ep_plain, F32,
                       name="lru_in")
            y, cst, hl = _lru_prompt(proj, *prm, n_seq, seq)
            xp = _mm(y, [(lru_w_o[j], 0)], d, _ep_residual, F32, full_extra=[xp], name="lru_out")
            lru_conv_p.append(cst)
            lru_h_p.append(hl)

            proj = _mm(_rmsnorm(xs, g_mix[i], BF16), [(lru_w_in[j], 0)], 2 * d, _ep_plain, F32,
                       name="lru_in_s")
            conv_rows = [state_lru_conv[j][:, r, :] for r in range(state_lru_conv.shape[2])]
            y, h_new = _lru_sample(proj, conv_rows, state_lru_h[j], *prm)
            xs = _mm(y, [(lru_w_o[j], 0)], d, _ep_residual, F32, full_extra=[xs], name="lru_out_s")
            lru_conv_s.append(jnp.concatenate(
                [state_lru_conv[j][:, 1:, :], proj[:, None, :d]], axis=1))
            lru_h_s.append(h_new)
        elif kind == 1:
            xn = _rmsnorm(xp, g_mix[i], F32)
            xp, st = _pool_prompt(xn, xp, pool_w[j], pool_b[j], pool_scale[j], n_seq, seq)
            pool_p.append(st)

            xn = _rmsnorm(xs, g_mix[i], F32)
            xs = _pool_sample(xn, xs, jnp.swapaxes(state_pool[j], 0, 1), pool_w[j], pool_b[j],
                              pool_scale[j], past_len)
            pool_s.append(jnp.concatenate([state_pool[j][:, 1:, :], xn[:, None, :]], axis=1))
        else:
            prm = (s5_a_re[j], s5_a_im[j], s5_b_re[j], s5_b_im[j], s5_c_re[j], s5_c_im[j],
                   s5_d[j], s5_log_dt[j], s5_w_glu[j], s5_b_glu[j])
            xp, fin = _s5_mixer(xp, _rmsnorm(xp, g_mix[i], F32), None, n_seq, prm, S5_CHUNK)
            re, im = _s5_state_unlayout(fin, g_all, p_state)
            s5_re_p.append(re)
            s5_im_p.append(im)

            h0 = _s5_state_layout(state_s5_re[j], state_s5_im[j])
            xs, fin = _s5_mixer(xs, _rmsnorm(xs, g_mix[i], F32), h0, bsz, prm, 1)
            re, im = _s5_state_unlayout(fin, g_all, p_state)
            s5_re_s.append(re)
            s5_im_s.append(im)

        mn = _rmsnorm(mem, g_mem[i], BF16)
        k = _mm(mn, [(w_k[i], 0)], d, _ep_plain, F32, name="mem_k")
        v = _mm(mn, [(w_v[i], 0)], d, _ep_plain, F32, name="mem_v")
        mem_k.append(k.reshape(n_seq, mem_tokens, heads, head_dim))
        mem_v.append(v.reshape(n_seq, mem_tokens, heads, head_dim))
        q = _mm(_rmsnorm(xp, g_xattn[i], BF16), [(w_q[i], 0)], d, _ep_plain, BF16, name="q")
        o = _attn_prompt(q, k.reshape(n_seq, mem_tokens, d), v.reshape(n_seq, mem_tokens, d),
                         n_seq, seq, heads)
        xp = _mm(o, [(w_o[i], 0)], d, _ep_residual, F32, full_extra=[xp], name="attn_out")

        q = _mm(_rmsnorm(xs, g_xattn[i], BF16), [(w_q[i], 0)], d, _ep_plain, F32, name="q_s")
        o = _attn_sample(q, cache_mem_k[i].reshape(bsz, mem_tokens, d),
                         cache_mem_v[i].reshape(bsz, mem_tokens, d), heads)
        xs = _mm(o, [(w_o[i], 0)], d, _ep_residual, F32, full_extra=[xs], name="attn_out_s")

        xp = _mlp(xp, g_mlp[i], wu_bf[i], wd_bf[i])
        xs = _mlp(xs, g_mlp[i], wu_bf[i], wd_bf[i])

    y_prompt = _rmsnorm(xp, g_final, F32).reshape(n_seq, seq, d)
    y_sample = _rmsnorm(xs, g_final, F32).reshape(bsz, 1, d)
    return (y_prompt, y_sample,
            jnp.stack(mem_k), jnp.stack(mem_v),
            jnp.stack(lru_conv_p), jnp.stack(lru_h_p), jnp.stack(pool_p),
            jnp.stack(s5_re_p), jnp.stack(s5_im_p),
            jnp.stack(lru_conv_s), jnp.stack(lru_h_s), jnp.stack(pool_s),
            jnp.stack(s5_re_s), jnp.stack(s5_im_s))
```

```python
import functools
import math

import jax
import jax.numpy as jnp
from jax import lax
from jax.experimental import pallas as pl
from jax.experimental.pallas import tpu as pltpu

F32 = jnp.float32
BF16 = jnp.bfloat16

SUBLANES = 8
LANES = 128
VMEM_LIMIT_BYTES = 56 * 1024 * 1024

RMS_EPS = 1e-6
LRU_C = 8.0
POOL_WINDOWS = (2, 4, 8, 16)
POOL_BUF = max(POOL_WINDOWS) - 1
S5_GROUP_DIM = 16
S5_CHUNK = 8
S5_GROUPS_PER_BLOCK = LANES // S5_GROUP_DIM


def _params(n_axes):
    return pltpu.CompilerParams(dimension_semantics=("arbitrary",) * n_axes,
                                vmem_limit_bytes=VMEM_LIMIT_BYTES)


def _rms(x, g):
    xf = x.astype(F32)
    inv = lax.rsqrt(jnp.mean(xf * xf, axis=-1, keepdims=True) + RMS_EPS)
    return xf * inv * g.astype(F32)


def _rmsnorm_kernel(x_ref, g_ref, o_ref):
    o_ref[...] = _rms(x_ref[...], g_ref[...]).astype(o_ref.dtype)


def _rmsnorm(x, g, out_dtype, tm=512):
    rows, d = x.shape
    tm = min(tm, rows)
    return pl.pallas_call(
        _rmsnorm_kernel,
        grid=(rows // tm,),
        in_specs=[pl.BlockSpec((tm, d), lambda i: (i, 0)),
                  pl.BlockSpec((1, d), lambda i: (0, 0))],
        out_specs=pl.BlockSpec((tm, d), lambda i: (i, 0)),
        out_shape=jax.ShapeDtypeStruct((rows, d), out_dtype),
        compiler_params=_params(1),
        name="rmsnorm",
    )(x, g.reshape(1, d))


def _mm_kernel(*refs, n_w, n_row, n_full, epilogue):
    x_ref = refs[0]
    w_refs = refs[1:1 + n_w]
    row_refs = refs[1 + n_w:1 + n_w + n_row]
    full_refs = refs[1 + n_w + n_row:1 + n_w + n_row + n_full]
    o_ref = refs[1 + n_w + n_row + n_full]
    wbf_refs = refs[2 + n_w + n_row + n_full:]

    @pl.when(pl.program_id(1) == 0)
    def _():
        for w_ref, wbf_ref in zip(w_refs, wbf_refs):
            wbf_ref[...] = w_ref[0].astype(BF16)

    xb = x_ref[...].astype(BF16)
    accs = [jnp.dot(xb, wbf_ref[...], preferred_element_type=F32) for wbf_ref in wbf_refs]
    out = epilogue(accs, [r[0] for r in row_refs], [f[...] for f in full_refs])
    o_ref[...] = out.astype(o_ref.dtype)


def _mm(x, ws, n_cols, epilogue, out_dtype, *, rows_extra=(), full_extra=(), tm=1024, tn=512,
        name="mm"):
    rows, k = x.shape
    tm = min(tm, rows)
    tn = min(tn, n_cols)
    in_specs = [pl.BlockSpec((tm, k), lambda j, i: (i, 0))]
    args = [x]
    for w, layer, off in ws:
        in_specs.append(pl.BlockSpec((1, k, tn), lambda j, i, l=layer, o=off // tn: (l, 0, o + j)))
        args.append(w)
    for v, layer, off in rows_extra:
        in_specs.append(pl.BlockSpec((1, 1, tn), lambda j, i, l=layer, o=off // tn: (l, 0, o + j)))
        args.append(v)
    for f in full_extra:
        in_specs.append(pl.BlockSpec((tm, tn), lambda j, i: (i, j)))
        args.append(f)
    kern = functools.partial(_mm_kernel, n_w=len(ws), n_row=len(rows_extra),
                             n_full=len(full_extra), epilogue=epilogue)
    return pl.pallas_call(
        kern,
        grid=(n_cols // tn, rows // tm),
        in_specs=in_specs,
        out_specs=pl.BlockSpec((tm, tn), lambda j, i: (i, j)),
        out_shape=jax.ShapeDtypeStruct((rows, n_cols), out_dtype),
        scratch_shapes=[pltpu.VMEM((k, tn), BF16) for _ in ws],
        compiler_params=_params(2),
        name=name,
    )(*args)


def _ep_plain(accs, rows, fulls):
    return accs[0]


def _ep_residual(accs, rows, fulls):
    return fulls[0] + accs[0]


def _ep_glu_residual(accs, rows, fulls):
    a = accs[0] + rows[0]
    b = accs[1] + rows[1]
    return fulls[0] + a * jax.nn.sigmoid(b)


def _mlp_kernel(x_ref, g_ref, wu_ref, wd_ref, o_ref, xn_ref):
    @pl.when(pl.program_id(1) == 0)
    def _():
        xv = x_ref[...]
        xn_ref[...] = _rms(xv, g_ref[...]).astype(BF16)
        o_ref[...] = xv

    h = jnp.dot(xn_ref[...], wu_ref[0], preferred_element_type=F32)
    h = jnp.maximum(h, 0.0)
    o_ref[...] += jnp.dot((h * h).astype(BF16), wd_ref[0], preferred_element_type=F32)


def _mlp(x, g, wu_bf, wd_bf, layer, tm=1024, tf=512):
    rows, d = x.shape
    dff = wu_bf.shape[2]
    tm = min(tm, rows)
    return pl.pallas_call(
        _mlp_kernel,
        grid=(rows // tm, dff // tf),
        in_specs=[pl.BlockSpec((tm, d), lambda i, f: (i, 0), pipeline_mode=pl.Buffered(1)),
                  pl.BlockSpec((1, d), lambda i, f: (0, 0)),
                  pl.BlockSpec((1, d, tf), lambda i, f: (layer, 0, f)),
                  pl.BlockSpec((1, tf, d), lambda i, f: (layer, f, 0))],
        out_specs=pl.BlockSpec((tm, d), lambda i, f: (i, 0)),
        out_shape=jax.ShapeDtypeStruct((rows, d), F32),
        scratch_shapes=[pltpu.VMEM((tm, d), BF16)],
        compiler_params=_params(2),
        name="mlp",
    )(x, g.reshape(1, d), wu_bf, wd_bf)


def _attn_prompt_kernel(q_ref, k_ref, v_ref, o_ref, *, scale):
    q = q_ref[...]
    k = k_ref[0].astype(BF16)
    s = lax.dot_general(q, k, (((1,), (1,)), ((), ())), preferred_element_type=F32) * scale
    m = jnp.max(s, axis=-1, keepdims=True)
    e = jnp.exp(s - m)
    p = e / jnp.sum(e, axis=-1, keepdims=True)
    o = jnp.dot(p.astype(BF16), v_ref[0].astype(BF16), preferred_element_type=F32)
    o_ref[...] = o.astype(o_ref.dtype)


def _attn_prompt(q, k, v, n_seq, seq, heads, tq=1024):
    rows, d = q.shape
    m = k.shape[1]
    hd = d // heads
    tq = min(tq, seq)
    tps = seq // tq
    kern = functools.partial(_attn_prompt_kernel, scale=hd ** -0.5)
    return pl.pallas_call(
        kern,
        grid=(n_seq, tps, heads),
        in_specs=[pl.BlockSpec((tq, hd), lambda b, i, h: (b * tps + i, h)),
                  pl.BlockSpec((1, m, hd), lambda b, i, h: (b, 0, h)),
                  pl.BlockSpec((1, m, hd), lambda b, i, h: (b, 0, h))],
        out_specs=pl.BlockSpec((tq, hd), lambda b, i, h: (b * tps + i, h)),
        out_shape=jax.ShapeDtypeStruct((rows, d), BF16),
        compiler_params=_params(3),
        name="attn_prompt",
    )(q, k, v)


def _attn_sample_kernel(q_ref, k_ref, v_ref, o_ref, *, scale, bt):
    for b in range(bt):
        q = q_ref[b]
        s = jnp.sum(k_ref[0, b] * q[None], axis=-1, keepdims=True) * scale
        mx = jnp.max(s, axis=0, keepdims=True)
        e = jnp.exp(s - mx)
        p = e / jnp.sum(e, axis=0, keepdims=True)
        o_ref[b] = jnp.sum(p * v_ref[0, b], axis=0)


def _attn_sample(q, cache_k, cache_v, layer, bt=2):
    bsz, heads, hd = q.shape
    m = cache_k.shape[2]
    kern = functools.partial(_attn_sample_kernel, scale=hd ** -0.5, bt=bt)
    kv_spec = pl.BlockSpec((1, bt, m, heads, hd), lambda i: (layer, i, 0, 0, 0))
    return pl.pallas_call(
        kern,
        grid=(bsz // bt,),
        in_specs=[pl.BlockSpec((bt, heads, hd), lambda i: (i, 0, 0)), kv_spec, kv_spec],
        out_specs=pl.BlockSpec((bt, heads, hd), lambda i: (i, 0, 0)),
        out_shape=jax.ShapeDtypeStruct((bsz, heads, hd), F32),
        compiler_params=_params(1),
        name="attn_sample",
    )(q, cache_k, cache_v)


def _lru_gates(xc, wa, ba, wi, bi, lam):
    xcb = xc.astype(BF16)
    r = jax.nn.sigmoid(jnp.dot(xcb, wa.astype(BF16), preferred_element_type=F32) + ba)
    ig = jax.nn.sigmoid(jnp.dot(xcb, wi.astype(BF16), preferred_element_type=F32) + bi)
    log_a = -LRU_C * r * jax.nn.softplus(-lam)
    a = jnp.exp(log_a)
    u = jnp.sqrt(-jnp.tanh(log_a) * (a * a + 1.0)) * (ig * xc)
    return a, u


def _lru_prompt_kernel(xb_ref, gt_ref, cw_ref, cb_ref, wa_ref, ba_ref, wi_ref, bi_ref, lam_ref,
                       y_ref, cst_ref, hl_ref, ext_ref, a_ref, u_ref, hs_ref, hc_ref,
                       *, tiles_per_seq, taps):
    tm, cb = xb_ref.shape
    halo = SUBLANES

    @pl.when(pl.program_id(1) % tiles_per_seq == 0)
    def _():
        ext_ref[0:halo, :] = jnp.zeros((halo, cb), F32)
        hc_ref[...] = jnp.zeros((1, cb), F32)

    ext_ref[halo:halo + tm, :] = xb_ref[...]
    cw = cw_ref[...]
    xc = cb_ref[...]
    for k in range(taps):
        start = halo - (taps - 1) + k
        xc = xc + cw[k:k + 1, :] * ext_ref[start:start + tm, :]
    cst_ref[0] = ext_ref[halo + tm - (taps - 1):halo + tm, :]
    ext_ref[0:halo, :] = ext_ref[tm:tm + halo, :]

    a, u = _lru_gates(xc, wa_ref[0], ba_ref[...], wi_ref[0], bi_ref[...], lam_ref[...])

    row = lax.broadcasted_iota(jnp.int32, (tm, cb), 0) % SUBLANES
    s = 1
    while s < SUBLANES:
        keep = row >= s
        u = u + a * jnp.where(keep, pltpu.roll(u, s, 0), 0.0)
        a = a * jnp.where(keep, pltpu.roll(a, s, 0), 1.0)
        s *= 2
    a_ref[...] = a
    u_ref[...] = u

    def link(g, h):
        off = pl.multiple_of(g * SUBLANES, SUBLANES)
        ag = a_ref[pl.ds(off, SUBLANES), :]
        ug = u_ref[pl.ds(off, SUBLANES), :]
        hs_ref[pl.ds(off, SUBLANES), :] = ag * h + ug
        return ag[SUBLANES - 1:SUBLANES, :] * h + ug[SUBLANES - 1:SUBLANES, :]

    h_last = lax.fori_loop(0, tm // SUBLANES, link, hc_ref[...], unroll=4)
    hc_ref[...] = h_last
    hl_ref[0] = h_last
    y_ref[...] = (hs_ref[...] * jax.nn.gelu(gt_ref[...])).astype(y_ref.dtype)


def _lru_prompt(proj, conv_w, conv_b, w_a, b_a, w_i, b_i, lam, n_seq, seq, tm=1024):
    rows, r2 = proj.shape
    r = r2 // 2
    nb, cb = w_a.shape[0], w_a.shape[1]
    taps = conv_w.shape[0]
    tm = min(tm, seq)
    tps = seq // tm
    kern = functools.partial(_lru_prompt_kernel, tiles_per_seq=tps, taps=taps)
    vec = lambda c, i: (0, c)
    y, cst, hl = pl.pallas_call(
        kern,
        grid=(nb, rows // tm),
        in_specs=[pl.BlockSpec((tm, cb), lambda c, i: (i, c)),
                  pl.BlockSpec((tm, cb), lambda c, i: (i, nb + c)),
                  pl.BlockSpec((taps, cb), vec),
                  pl.BlockSpec((1, cb), vec),
                  pl.BlockSpec((1, cb, cb), lambda c, i: (c, 0, 0)),
                  pl.BlockSpec((1, cb), vec),
                  pl.BlockSpec((1, cb, cb), lambda c, i: (c, 0, 0)),
                  pl.BlockSpec((1, cb), vec),
                  pl.BlockSpec((1, cb), vec)],
        out_specs=[pl.BlockSpec((tm, cb), lambda c, i: (i, c)),
                   pl.BlockSpec((1, taps - 1, cb), lambda c, i: (i // tps, 0, c)),
                   pl.BlockSpec((1, 1, cb), lambda c, i: (i // tps, 0, c))],
        out_shape=[jax.ShapeDtypeStruct((rows, r), BF16),
                   jax.ShapeDtypeStruct((n_seq, taps - 1, r), F32),
                   jax.ShapeDtypeStruct((n_seq, 1, r), F32)],
        scratch_shapes=[pltpu.VMEM((tm + SUBLANES, cb), F32),
                        pltpu.VMEM((tm, cb), F32),
                        pltpu.VMEM((tm, cb), F32),
                        pltpu.VMEM((tm, cb), F32),
                        pltpu.VMEM((1, cb), F32)],
        compiler_params=_params(2),
        name="lru_prompt",
    )(proj, proj, conv_w, conv_b.reshape(1, r), w_a, b_a.reshape(1, r), w_i, b_i.reshape(1, r),
      lam.reshape(1, r))
    return y, cst, hl.reshape(n_seq, r)


def _lru_sample_kernel(xb_ref, gt_ref, c0_ref, c1_ref, c2_ref, h0_ref, cw_ref, cb_ref,
                       wa_ref, ba_ref, wi_ref, bi_ref, lam_ref, y_ref, h_ref):
    cw = cw_ref[...]
    xc = cb_ref[...]
    for k, c_ref in enumerate((c0_ref, c1_ref, c2_ref, xb_ref)):
        xc = xc + cw[k:k + 1, :] * c_ref[...]
    a, u = _lru_gates(xc, wa_ref[0], ba_ref[...], wi_ref[0], bi_ref[...], lam_ref[...])
    h = a * h0_ref[...] + u
    h_ref[...] = h
    y_ref[...] = (h * jax.nn.gelu(gt_ref[...])).astype(y_ref.dtype)


def _lru_sample(proj, conv_rows, h0, conv_w, conv_b, w_a, b_a, w_i, b_i, lam):
    bsz, r2 = proj.shape
    r = r2 // 2
    nb, cb = w_a.shape[0], w_a.shape[1]
    taps = conv_w.shape[0]
    blk = lambda c: (0, c)
    return pl.pallas_call(
        _lru_sample_kernel,
        grid=(nb,),
        in_specs=[pl.BlockSpec((bsz, cb), blk),
                  pl.BlockSpec((bsz, cb), lambda c: (0, nb + c)),
                  pl.BlockSpec((bsz, cb), blk), pl.BlockSpec((bsz, cb), blk),
                  pl.BlockSpec((bsz, cb), blk), pl.BlockSpec((bsz, cb), blk),
                  pl.BlockSpec((taps, cb), blk), pl.BlockSpec((1, cb), blk),
                  pl.BlockSpec((1, cb, cb), lambda c: (c, 0, 0)), pl.BlockSpec((1, cb), blk),
                  pl.BlockSpec((1, cb, cb), lambda c: (c, 0, 0)), pl.BlockSpec((1, cb), blk),
                  pl.BlockSpec((1, cb), blk)],
        out_specs=[pl.BlockSpec((bsz, cb), blk), pl.BlockSpec((bsz, cb), blk)],
        out_shape=[jax.ShapeDtypeStruct((bsz, r), BF16), jax.ShapeDtypeStruct((bsz, r), F32)],
        compiler_params=_params(1),
        name="lru_sample",
    )(proj, proj, conv_rows[0], conv_rows[1], conv_rows[2], h0, conv_w, conv_b.reshape(1, r),
      w_a, b_a.reshape(1, r), w_i, b_i.reshape(1, r), lam.reshape(1, r))


def _pool_prompt_kernel(xn_ref, x_ref, w_ref, b_ref, sc_ref, o_ref, st_ref, ext_ref, wbf_ref,
                        *, tiles_per_seq, windows):
    tm, d = xn_ref.shape
    gd = d // len(windows)
    halo = 2 * SUBLANES
    i = pl.program_id(0)

    @pl.when(i == 0)
    def _():
        wbf_ref[...] = w_ref[...].astype(BF16)

    @pl.when(i % tiles_per_seq == 0)
    def _():
        ext_ref[0:halo, :] = jnp.zeros((halo, d), F32)

    ext_ref[halo:halo + tm, :] = xn_ref[...]
    pos = (i % tiles_per_seq) * tm + lax.broadcasted_iota(jnp.int32, (tm, 1), 0)
    for g, w in enumerate(windows):
        sl = slice(g * gd, (g + 1) * gd)
        s = ext_ref[:, sl]
        shift = 1
        while shift < w:
            s = s + pltpu.roll(s, shift, 0)
            shift *= 2
        cnt = jnp.minimum(pos + 1, w).astype(F32)
        pooled = s[halo:, :] / cnt
        diff = (pooled - xn_ref[:, sl]).astype(BF16)
        mixed = jnp.dot(diff, wbf_ref[g], preferred_element_type=F32) + b_ref[:, sl]
        o_ref[:, sl] = x_ref[:, sl] + mixed * sc_ref[:, sl]
    st_ref[0] = ext_ref[halo + tm - POOL_BUF:halo + tm, :]
    ext_ref[0:halo, :] = ext_ref[tm:tm + halo, :]


def _pool_prompt(xn, x, w, b, scale, n_seq, seq, tm=512):
    rows, d = xn.shape
    tm = min(tm, seq)
    tps = seq // tm
    ng, gd = w.shape[0], w.shape[1]
    kern = functools.partial(_pool_prompt_kernel, tiles_per_seq=tps, windows=POOL_WINDOWS)
    return pl.pallas_call(
        kern,
        grid=(rows // tm,),
        in_specs=[pl.BlockSpec((tm, d), lambda i: (i, 0)),
                  pl.BlockSpec((tm, d), lambda i: (i, 0)),
                  pl.BlockSpec((ng, gd, gd), lambda i: (0, 0, 0)),
                  pl.BlockSpec((1, d), lambda i: (0, 0)),
                  pl.BlockSpec((1, d), lambda i: (0, 0))],
        out_specs=[pl.BlockSpec((tm, d), lambda i: (i, 0)),
                   pl.BlockSpec((1, POOL_BUF, d), lambda i: (i // tps, 0, 0))],
        out_shape=[jax.ShapeDtypeStruct((rows, d), F32),
                   jax.ShapeDtypeStruct((n_seq, POOL_BUF, d), F32)],
        scratch_shapes=[pltpu.VMEM((tm + 2 * SUBLANES, d), F32),
                        pltpu.VMEM((ng, gd, gd), BF16)],
        compiler_params=_params(1),
        name="pool_prompt",
    )(xn, x, w, b.reshape(1, d), scale.reshape(1, d))


def _pool_sample_kernel(xn_ref, x_ref, st_ref, w_ref, b_ref, sc_ref, o_ref, *, window, pos0):
    xn = xn_ref[...]
    s = xn
    for r in range(POOL_BUF - (window - 1), POOL_BUF):
        s = s + st_ref[r]
    pooled = s / float(min(pos0 + 1, window))
    diff = (pooled - xn).astype(BF16)
    mixed = jnp.dot(diff, w_ref[0].astype(BF16), preferred_element_type=F32) + b_ref[...]
    o_ref[...] = x_ref[...] + mixed * sc_ref[...]


def _pool_sample(xn, x, state_t, w, b, scale, pos0):
    bsz, d = xn.shape
    ng, gd = w.shape[0], w.shape[1]
    outs = []
    for g, window in enumerate(POOL_WINDOWS):
        kern = functools.partial(_pool_sample_kernel, window=window, pos0=pos0)
        blk = lambda i, g=g: (0, g)
        outs.append(pl.pallas_call(
            kern,
            grid=(1,),
            in_specs=[pl.BlockSpec((bsz, gd), blk), pl.BlockSpec((bsz, gd), blk),
                      pl.BlockSpec((POOL_BUF, bsz, gd), lambda i, g=g: (0, 0, g)),
                      pl.BlockSpec((1, gd, gd), lambda i, g=g: (g, 0, 0)),
                      pl.BlockSpec((1, gd), blk), pl.BlockSpec((1, gd), blk)],
            out_specs=pl.BlockSpec((bsz, gd), lambda i: (0, 0)),
            out_shape=jax.ShapeDtypeStruct((bsz, gd), F32),
            compiler_params=_params(1),
            name="pool_sample",
        )(xn, x, state_t, w, b.reshape(1, d), scale.reshape(1, d)))
    return jnp.concatenate(outs, axis=-1)


def _s5_matrices(a_re, a_im, b_re, b_im, c_re, c_im, log_dt, chunk):
    g_all, p = a_re.shape
    c = b_re.shape[-1]
    gpb = S5_GROUPS_PER_BLOCK
    nb = g_all // gpb
    dt = jnp.exp(log_dt.astype(F32))[:, None]
    mag = jnp.exp(a_re * dt)
    lr, li = mag * jnp.cos(a_im * dt), mag * jnp.sin(a_im * dt)
    den = a_re * a_re + a_im * a_im
    q_re = ((lr - 1.0) * a_re + li * a_im) / den
    q_im = (li * a_re - (lr - 1.0) * a_im) / den
    bb_re = q_re[..., None] * b_re - q_im[..., None] * b_im
    bb_im = q_re[..., None] * b_im + q_im[..., None] * b_re

    def cmul(xr, xi, yr, yi):
        return xr * yr - xi * yi, xr * yi + xi * yr

    pw = [(jnp.ones_like(lr), jnp.zeros_like(lr))]
    for _ in range(chunk):
        pw.append(cmul(pw[-1][0], pw[-1][1], lr, li))
    hi = lax.Precision.HIGHEST
    lane_g = jnp.arange(LANES) // c
    state_g = jnp.arange(gpb * p) // p
    mask_cc = (lane_g[:, None] == lane_g[None, :]).astype(F32)
    mask_cp = (lane_g[:, None] == state_g[None, :]).astype(F32)
    spread = jnp.tile(jnp.eye(p, dtype=F32), (1, gpb))

    def lanes_by_state(x):
        return x.transpose(0, 2, 1).reshape(nb, LANES, p)

    def state_by_lanes(x):
        return x.transpose(2, 0, 1).reshape(p, nb, LANES).transpose(1, 0, 2)

    def c_lam(tau):
        return cmul(c_re, c_im, pw[tau][0][:, None, :], pw[tau][1][:, None, :])

    bb_rows = jnp.concatenate([lanes_by_state(bb_re), -lanes_by_state(bb_im)], axis=-1)
    blocks = []
    for tau in range(chunk):
        wr, wi = c_lam(tau)
        w_cols = jnp.concatenate([state_by_lanes(wr), state_by_lanes(wi)], axis=1)
        blocks.append(mask_cc * jnp.einsum('brk,bkc->brc', bb_rows, w_cols, precision=hi))
    zero = jnp.zeros_like(blocks[0])
    m_in = jnp.stack([jnp.concatenate([blocks[t - s] if t >= s else zero for t in range(chunk)],
                                      axis=-1) for s in range(chunk)], axis=1)
    m_in = m_in.reshape(nb, chunk * LANES, chunk * LANES).astype(BF16)

    rows_t = []
    for t in range(chunk):
        pr, pi = pw[chunk - 1 - t]
        er, ei = cmul(pr[..., None], pi[..., None], bb_re, bb_im)
        halves = [mask_cp * jnp.einsum('brp,pq->brq', lanes_by_state(e), spread, precision=hi)
                  for e in (er, ei)]
        rows_t.append(jnp.concatenate(halves, axis=-1))
    m_state = jnp.stack(rows_t, axis=1).reshape(nb, chunk * LANES, 2 * gpb * p).astype(BF16)

    cols_t = []
    for t in range(chunk):
        wr, wi = c_lam(t + 1)
        halves = [mask_cp.T * jnp.einsum('qp,bpc->bqc', spread.T, state_by_lanes(w), precision=hi)
                  for w in (wr, -wi)]
        cols_t.append(jnp.concatenate(halves, axis=1))
    m_out = jnp.concatenate(cols_t, axis=-1).astype(BF16)

    qr, qi = pw[chunk]
    rows = [(jnp.ones_like(qr), jnp.zeros_like(qr))]
    for _ in range(SUBLANES):
        rows.append(cmul(rows[-1][0], rows[-1][1], qr, qi))
    table = jnp.stack([_s5_state_layout(r_, i_) for r_, i_ in rows])
    table = jnp.concatenate([table, jnp.zeros((2 * SUBLANES - table.shape[0], table.shape[1]), F32)])
    return m_in, m_state, m_out, table


def _s5_state_layout(re, im):
    lead = re.shape[:-2]
    g_all, p = re.shape[-2:]
    nb = g_all // S5_GROUPS_PER_BLOCK
    st = jnp.stack([re.reshape(lead + (nb, S5_GROUPS_PER_BLOCK * p)),
                    im.reshape(lead + (nb, S5_GROUPS_PER_BLOCK * p))], axis=-2)
    return st.reshape(lead + (2 * g_all * p,))


def _s5_state_unlayout(flat, g_all, p):
    lead = flat.shape[:-1]
    nb = g_all // S5_GROUPS_PER_BLOCK
    st = flat.reshape(lead + (nb, 2, S5_GROUPS_PER_BLOCK * p))
    return st[..., 0, :].reshape(lead + (g_all, p)), st[..., 1, :].reshape(lead + (g_all, p))


def _s5_lhs(u_refs):
    return jnp.concatenate([u[...] for u in u_refs], axis=1).astype(BF16)


def _s5_state_in_kernel(*refs, chunk):
    u_refs, m_ref, e_ref = refs[:chunk], refs[chunk], refs[chunk + 1]
    e_ref[...] = jnp.dot(_s5_lhs(u_refs), m_ref[0], preferred_element_type=F32)


def _s5_state_in(xnv, m_state, chunk, d, tr=256):
    r = xnv.shape[0]
    nb, _, sw = m_state.shape
    tr = min(tr, r)
    lb = d // LANES
    kern = functools.partial(_s5_state_in_kernel, chunk=chunk)
    return pl.pallas_call(
        kern,
        grid=(nb, r // tr),
        in_specs=[pl.BlockSpec((tr, LANES), lambda b, i, t=t: (i, t * lb + b)) for t in range(chunk)]
        + [pl.BlockSpec((1, chunk * LANES, sw), lambda b, i: (b, 0, 0))],
        out_specs=pl.BlockSpec((tr, sw), lambda b, i: (i, b)),
        out_shape=jax.ShapeDtypeStruct((r, nb * sw), F32),
        compiler_params=_params(2),
        name="s5_state_in",
    )(*([xnv] * chunk), m_state)


def _s5_scan_kernel(e_ref, tab_ref, h_ref, fin_ref):
    rps, sw = e_ref.shape
    half = sw // 2
    er, ei = e_ref[:, :half], e_ref[:, half:]
    row = lax.broadcasted_iota(jnp.int32, (rps, half), 0) % SUBLANES
    s = 1
    while s < SUBLANES:
        keep = row >= s
        pr, pi = tab_ref[s:s + 1, :half], tab_ref[s:s + 1, half:]
        sr = jnp.where(keep, pltpu.roll(er, s, 0), 0.0)
        si = jnp.where(keep, pltpu.roll(ei, s, 0), 0.0)
        er, ei = er + pr * sr - pi * si, ei + pr * si + pi * sr
        s *= 2
    xr = jnp.where(row >= 1, pltpu.roll(er, 1, 0), 0.0)
    xi = jnp.where(row >= 1, pltpu.roll(ei, 1, 0), 0.0)
    ar, ai = tab_ref[0:SUBLANES, :half], tab_ref[0:SUBLANES, half:]
    l8r, l8i = tab_ref[SUBLANES:SUBLANES + 1, :half], tab_ref[SUBLANES:SUBLANES + 1, half:]
    cr = jnp.zeros((1, half), F32)
    ci = jnp.zeros((1, half), F32)
    for g in range(rps // SUBLANES):
        lo, hi = g * SUBLANES, (g + 1) * SUBLANES
        h_ref[lo:hi, :half] = ar * cr - ai * ci + xr[lo:hi]
        h_ref[lo:hi, half:] = ar * ci + ai * cr + xi[lo:hi]
        cr, ci = (l8r * cr - l8i * ci + er[hi - 1:hi], l8r * ci + l8i * cr + ei[hi - 1:hi])
    fin_ref[0, :, :half] = cr
    fin_ref[0, :, half:] = ci


def _s5_scan(e, table, n_seq, sw):
    r, width = e.shape
    rps = r // n_seq
    return pl.pallas_call(
        _s5_scan_kernel,
        grid=(width // sw, n_seq),
        in_specs=[pl.BlockSpec((rps, sw), lambda c, b: (b, c)),
                  pl.BlockSpec((2 * SUBLANES, sw), lambda c, b: (0, c))],
        out_specs=[pl.BlockSpec((rps, sw), lambda c, b: (b, c)),
                   pl.BlockSpec((1, 1, sw), lambda c, b: (b, 0, c))],
        out_shape=[jax.ShapeDtypeStruct((r, width), F32),
                   jax.ShapeDtypeStruct((n_seq, 1, width), F32)],
        compiler_params=_params(2),
        name="s5_scan",
    )(e, table)


def _s5_step_kernel(e_ref, h0_ref, tab_ref, o_ref):
    half = e_ref.shape[1] // 2
    lr, li = tab_ref[1:2, :half], tab_ref[1:2, half:]
    hr, hi = h0_ref[:, :half], h0_ref[:, half:]
    o_ref[:, :half] = lr * hr - li * hi + e_ref[:, :half]
    o_ref[:, half:] = lr * hi + li * hr + e_ref[:, half:]


def _s5_step(e, h0, table, sw):
    bsz, width = e.shape
    return pl.pallas_call(
        _s5_step_kernel,
        grid=(width // sw,),
        in_specs=[pl.BlockSpec((bsz, sw), lambda c: (0, c)),
                  pl.BlockSpec((bsz, sw), lambda c: (0, c)),
                  pl.BlockSpec((2 * SUBLANES, sw), lambda c: (0, c))],
        out_specs=pl.BlockSpec((bsz, sw), lambda c: (0, c)),
        out_shape=jax.ShapeDtypeStruct((bsz, width), F32),
        compiler_params=_params(1),
        name="s5_step",
    )(e, h0, table)


def _s5_out_kernel(*refs, chunk):
    u_refs = refs[:chunk]
    h_ref, mi_ref, mo_ref = refs[chunk:chunk + 3]
    y_refs = refs[chunk + 3:]
    y = jnp.dot(_s5_lhs(u_refs), mi_ref[0], preferred_element_type=F32)
    y = y + jnp.dot(h_ref[...].astype(BF16), mo_ref[0], preferred_element_type=F32)
    for t, y_ref in enumerate(y_refs):
        y_ref[...] = y[:, t * LANES:(t + 1) * LANES]


def _s5_out(xnv, h, m_in, m_out, chunk, d, tr=256):
    r = xnv.shape[0]
    nb, sw, _ = m_out.shape
    tr = min(tr, r)
    lb = d // LANES
    kern = functools.partial(_s5_out_kernel, chunk=chunk)
    return pl.pallas_call(
        kern,
        grid=(nb, r // tr),
        in_specs=[pl.BlockSpec((tr, LANES), lambda b, i, t=t: (i, t * lb + b)) for t in range(chunk)]
        + [pl.BlockSpec((tr, sw), lambda b, i: (i, b)),
           pl.BlockSpec((1, chunk * LANES, chunk * LANES), lambda b, i: (b, 0, 0)),
           pl.BlockSpec((1, sw, chunk * LANES), lambda b, i: (b, 0, 0))],
        out_specs=[pl.BlockSpec((tr, LANES), lambda b, i: (i, b)) for _ in range(chunk)],
        out_shape=[jax.ShapeDtypeStruct((r, d), F32) for _ in range(chunk)],
        compiler_params=_params(2),
        name="s5_out",
    )(*([xnv] * chunk), h, m_in, m_out)


def _s5_gelu_kernel(y_ref, xn_ref, d_ref, z_ref):
    z_ref[...] = jax.nn.gelu(y_ref[...] + d_ref[...] * xn_ref[...]).astype(z_ref.dtype)


def _s5_gelu(y, xn, dvec, tm=512):
    rows, d = xn.shape
    tm = min(tm, rows)
    blk = pl.BlockSpec((tm, d), lambda i: (i, 0))
    return pl.pallas_call(
        _s5_gelu_kernel,
        grid=(rows // tm,),
        in_specs=[blk, blk, pl.BlockSpec((1, d), lambda i: (0, 0))],
        out_specs=blk,
        out_shape=jax.ShapeDtypeStruct((rows, d), BF16),
        compiler_params=_params(1),
        name="s5_gelu",
    )(y, xn, dvec.reshape(1, d))


def _s5_mixer(x, xn, h0_flat, n_seq, prm, w_glu, b_glu, layer, chunk):
    a_re, a_im, b_re, b_im, c_re, c_im, dvec, log_dt = prm
    rows, d = xn.shape
    m_in, m_state, m_out, table = _s5_matrices(a_re, a_im, b_re, b_im, c_re, c_im, log_dt, chunk)
    sw = m_state.shape[-1]
    xnv = xn.reshape(rows // chunk, chunk * d)
    e = _s5_state_in(xnv, m_state, chunk, d)
    if h0_flat is None:
        h, fin = _s5_scan(e, table, n_seq, sw)
        fin = fin.reshape(n_seq, -1)
    else:
        h = h0_flat
        fin = _s5_step(e, h0_flat, table, sw)
    ys = _s5_out(xnv, h, m_in, m_out, chunk, d)
    y = jnp.stack(ys, axis=1).reshape(rows, d)
    z = _s5_gelu(y, xn, dvec)
    b_glu3 = b_glu.reshape(b_glu.shape[0], 1, 2 * d)
    x_new = _mm(z, [(w_glu, layer, 0), (w_glu, layer, d)], d, _ep_glu_residual, F32,
                rows_extra=[(b_glu3, layer, 0), (b_glu3, layer, d)],
                full_extra=[x], name="s5_glu")
    return x_new, fin


def kernel(x_prompt, x_sample, cache_mem_k, cache_mem_v, state_lru_conv, state_lru_h, state_pool, state_s5_re, state_s5_im, mem_prompt, g_mix, g_xattn, g_mem, g_mlp, g_final, w_q, w_k, w_v, w_o, w_up, w_down, lru_w_in, lru_conv_w, lru_conv_b, lru_w_a, lru_b_a, lru_w_i, lru_b_i, lru_lambda, lru_w_o, pool_w, pool_b, pool_scale, s5_a_re, s5_a_im, s5_b_re, s5_b_im, s5_c_re, s5_c_im, s5_d, s5_log_dt, s5_w_glu, s5_b_glu):
    n_seq, seq, d = x_prompt.shape
    bsz, dec_seq, _ = x_sample.shape
    assert dec_seq == 1, "the sample group advances one token per request"
    depth = g_mix.shape[0]
    mem_tokens = mem_prompt.shape[1]
    heads, head_dim = cache_mem_k.shape[3], cache_mem_k.shape[4]
    g_all, p_state = s5_a_re.shape[1], s5_a_re.shape[2]
    past_len = 16384

    xp = x_prompt.reshape(n_seq * seq, d)
    xs = x_sample.reshape(bsz, d)
    mem = mem_prompt.reshape(n_seq * mem_tokens, d)
    wu_bf = w_up.astype(BF16)
    wd_bf = w_down.astype(BF16)

    mem_k, mem_v = [], []
    lru_conv_p, lru_h_p, lru_conv_s, lru_h_s = [], [], [], []
    pool_p, pool_s = [], []
    s5_re_p, s5_im_p, s5_re_s, s5_im_s = [], [], [], []

    for i in range(depth):
        kind, j = i % 3, i // 3
        if kind == 0:
            prm = (lru_conv_w[j], lru_conv_b[j], lru_w_a[j], lru_b_a[j], lru_w_i[j], lru_b_i[j],
                   lru_lambda[j])
            proj = _mm(_rmsnorm(xp, g_mix[i], BF16), [(lru_w_in, j, 0)], 2 * d, _ep_plain, F32,
                       name="lru_in")
            y, cst, hl = _lru_prompt(proj, *prm, n_seq, seq)
            xp = _mm(y, [(lru_w_o, j, 0)], d, _ep_residual, F32, full_extra=[xp], name="lru_out")
            lru_conv_p.append(cst)
            lru_h_p.append(hl)

            proj = _mm(_rmsnorm(xs, g_mix[i], BF16), [(lru_w_in, j, 0)], 2 * d, _ep_plain, F32,
                       name="lru_in_s")
            conv_rows = [state_lru_conv[j][:, r, :] for r in range(state_lru_conv.shape[2])]
            y, h_new = _lru_sample(proj, conv_rows, state_lru_h[j], *prm)
            xs = _mm(y, [(lru_w_o, j, 0)], d, _ep_residual, F32, full_extra=[xs], name="lru_out_s")
            lru_conv_s.append(jnp.concatenate(
                [state_lru_conv[j][:, 1:, :], proj[:, None, :d]], axis=1))
            lru_h_s.append(h_new)
        elif kind == 1:
            xn = _rmsnorm(xp, g_mix[i], F32)
            xp, st = _pool_prompt(xn, xp, pool_w[j], pool_b[j], pool_scale[j], n_seq, seq)
            pool_p.append(st)

            xn = _rmsnorm(xs, g_mix[i], F32)
            xs = _pool_sample(xn, xs, jnp.swapaxes(state_pool[j], 0, 1), pool_w[j], pool_b[j],
                              pool_scale[j], past_len)
            pool_s.append(jnp.concatenate([state_pool[j][:, 1:, :], xn[:, None, :]], axis=1))
        else:
            prm = (s5_a_re[j], s5_a_im[j], s5_b_re[j], s5_b_im[j], s5_c_re[j], s5_c_im[j],
                   s5_d[j], s5_log_dt[j])
            xp, fin = _s5_mixer(xp, _rmsnorm(xp, g_mix[i], F32), None, n_seq, prm,
                                s5_w_glu, s5_b_glu, j, S5_CHUNK)
            re, im = _s5_state_unlayout(fin, g_all, p_state)
            s5_re_p.append(re)
            s5_im_p.append(im)

            h0 = _s5_state_layout(state_s5_re[j], state_s5_im[j])
            xs, fin = _s5_mixer(xs, _rmsnorm(xs, g_mix[i], F32), h0, bsz, prm,
                                s5_w_glu, s5_b_glu, j, 1)
            re, im = _s5_state_unlayout(fin, g_all, p_state)
            s5_re_s.append(re)
            s5_im_s.append(im)

        mn = _rmsnorm(mem, g_mem[i], BF16)
        k = _mm(mn, [(w_k, i, 0)], d, _ep_plain, F32, name="mem_k")
        v = _mm(mn, [(w_v, i, 0)], d, _ep_plain, F32, name="mem_v")
        mem_k.append(k.reshape(n_seq, mem_tokens, heads, head_dim))
        mem_v.append(v.reshape(n_seq, mem_tokens, heads, head_dim))
        q = _mm(_rmsnorm(xp, g_xattn[i], BF16), [(w_q, i, 0)], d, _ep_plain, BF16, name="q")
        o = _attn_prompt(q, k.reshape(n_seq, mem_tokens, d), v.reshape(n_seq, mem_tokens, d),
                         n_seq, seq, heads)
        xp = _mm(o, [(w_o, i, 0)], d, _ep_residual, F32, full_extra=[xp], name="attn_out")

        q = _mm(_rmsnorm(xs, g_xattn[i], BF16), [(w_q, i, 0)], d, _ep_plain, F32, name="q_s")
        o = _attn_sample(q.reshape(bsz, heads, head_dim), cache_mem_k, cache_mem_v, i)
        xs = _mm(o.reshape(bsz, d), [(w_o, i, 0)], d, _ep_residual, F32, full_extra=[xs],
                 name="attn_out_s")

        xp = _mlp(xp, g_mlp[i], wu_bf, wd_bf, i)
        xs = _mlp(xs, g_mlp[i], wu_bf, wd_bf, i)

    y_prompt = _rmsnorm(xp, g_final, F32).reshape(n_seq, seq, d)
    y_sample = _rmsnorm(xs, g_final, F32).reshape(bsz, 1, d)
    return (y_prompt, y_sample,
            jnp.stack(mem_k), jnp.stack(mem_v),
            jnp.stack(lru_conv_p), jnp.stack(lru_h_p), jnp.stack(pool_p),
            jnp.stack(s5_re_p), jnp.stack(s5_im_p),
            jnp.stack(lru_conv_s), jnp.stack(lru_h_s), jnp.stack(pool_s),
            jnp.stack(s5_re_s), jnp.stack(s5_im_s))
```

```python
import functools
import math

import jax
import jax.numpy as jnp
from jax import lax
from jax.experimental import pallas as pl
from jax.experimental.pallas import tpu as pltpu

F32 = jnp.float32
BF16 = jnp.bfloat16

SUBLANES = 8
LANES = 128
VMEM_LIMIT_BYTES = 56 * 1024 * 1024

RMS_EPS = 1e-6
LRU_C = 8.0
POOL_WINDOWS = (2, 4, 8, 16)
POOL_BUF = max(POOL_WINDOWS) - 1
S5_GROUP_DIM = 16
S5_CHUNK = 8
S5_GROUPS_PER_BLOCK = LANES // S5_GROUP_DIM


def _params(n_axes):
    return pltpu.CompilerParams(dimension_semantics=("arbitrary",) * n_axes,
                                vmem_limit_bytes=VMEM_LIMIT_BYTES)


def _rms(x, g):
    xf = x.astype(F32)
    inv = lax.rsqrt(jnp.mean(xf * xf, axis=-1, keepdims=True) + RMS_EPS)
    return xf * inv * g.astype(F32)


def _rmsnorm_kernel(x_ref, g_ref, o_ref):
    o_ref[...] = _rms(x_ref[...], g_ref[...]).astype(o_ref.dtype)


def _rmsnorm(x, g, out_dtype, tm=512):
    rows, d = x.shape
    tm = min(tm, rows)
    return pl.pallas_call(
        _rmsnorm_kernel,
        grid=(rows // tm,),
        in_specs=[pl.BlockSpec((tm, d), lambda i: (i, 0)),
                  pl.BlockSpec((1, d), lambda i: (0, 0))],
        out_specs=pl.BlockSpec((tm, d), lambda i: (i, 0)),
        out_shape=jax.ShapeDtypeStruct((rows, d), out_dtype),
        compiler_params=_params(1),
        name="rmsnorm",
    )(x, g.reshape(1, d))


def _mm_kernel(*refs, n_w, n_row, n_full, has_norm, epilogue):
    x_ref = refs[0]
    refs = refs[1:]
    g_ref = refs[0] if has_norm else None
    refs = refs[1:] if has_norm else refs
    w_refs = refs[:n_w]
    row_refs = refs[n_w:n_w + n_row]
    full_refs = refs[n_w + n_row:n_w + n_row + n_full]
    o_ref = refs[n_w + n_row + n_full]
    wbf_refs = refs[1 + n_w + n_row + n_full:]

    @pl.when(pl.program_id(1) == 0)
    def _():
        for w_ref, wbf_ref in zip(w_refs, wbf_refs):
            wbf_ref[...] = w_ref[0].astype(BF16)

    if has_norm:
        xb = _rms(x_ref[...], g_ref[...]).astype(BF16)
    else:
        xb = x_ref[...].astype(BF16)
    accs = [jnp.dot(xb, wbf_ref[...], preferred_element_type=F32) for wbf_ref in wbf_refs]
    out = epilogue(accs, [r[0] for r in row_refs], [f[...] for f in full_refs])
    o_ref[...] = out.astype(o_ref.dtype)


def _mm(x, ws, n_cols, epilogue, out_dtype, *, norm_g=None, rows_extra=(), full_extra=(),
        tm=1024, tn=512, single_w=False, name="mm"):
    rows, k = x.shape
    tm = min(tm, rows)
    tn = min(tn, n_cols)
    in_specs = [pl.BlockSpec((tm, k), lambda j, i: (i, 0))]
    args = [x]
    if norm_g is not None:
        in_specs.append(pl.BlockSpec((1, k), lambda j, i: (0, 0)))
        args.append(norm_g.reshape(1, k))
    w_mode = dict(pipeline_mode=pl.Buffered(1)) if single_w else {}
    for w, layer, off in ws:
        in_specs.append(pl.BlockSpec((1, k, tn), lambda j, i, l=layer, o=off // tn: (l, 0, o + j),
                                     **w_mode))
        args.append(w)
    for v, layer, off in rows_extra:
        in_specs.append(pl.BlockSpec((1, 1, tn), lambda j, i, l=layer, o=off // tn: (l, 0, o + j)))
        args.append(v)
    for f in full_extra:
        in_specs.append(pl.BlockSpec((tm, tn), lambda j, i: (i, j)))
        args.append(f)
    kern = functools.partial(_mm_kernel, n_w=len(ws), n_row=len(rows_extra),
                             n_full=len(full_extra), has_norm=norm_g is not None,
                             epilogue=epilogue)
    return pl.pallas_call(
        kern,
        grid=(n_cols // tn, rows // tm),
        in_specs=in_specs,
        out_specs=pl.BlockSpec((tm, tn), lambda j, i: (i, j)),
        out_shape=jax.ShapeDtypeStruct((rows, n_cols), out_dtype),
        scratch_shapes=[pltpu.VMEM((k, tn), BF16) for _ in ws],
        compiler_params=_params(2),
        name=name,
    )(*args)


def _ep_plain(accs, rows, fulls):
    return accs[0]


def _ep_residual(accs, rows, fulls):
    return fulls[0] + accs[0]


def _ep_glu_residual(accs, rows, fulls):
    a = accs[0] + rows[0]
    b = accs[1] + rows[1]
    return fulls[0] + a * jax.nn.sigmoid(b)


def _mlp_kernel(x_ref, g_ref, wu_ref, wd_ref, o_ref, xn_ref):
    @pl.when(pl.program_id(1) == 0)
    def _():
        xv = x_ref[...]
        xn_ref[...] = _rms(xv, g_ref[...]).astype(BF16)
        o_ref[...] = xv

    h = jnp.dot(xn_ref[...], wu_ref[...], preferred_element_type=F32)
    h = jnp.maximum(h, 0.0)
    o_ref[...] += jnp.dot((h * h).astype(BF16), wd_ref[...], preferred_element_type=F32)


def _mlp(x, g, wu_bf, wd_bf, tm=1024, tf=512):
    rows, d = x.shape
    dff = wu_bf.shape[1]
    tm = min(tm, rows)
    return pl.pallas_call(
        _mlp_kernel,
        grid=(rows // tm, dff // tf),
        in_specs=[pl.BlockSpec((tm, d), lambda i, f: (i, 0), pipeline_mode=pl.Buffered(1)),
                  pl.BlockSpec((1, d), lambda i, f: (0, 0)),
                  pl.BlockSpec((d, tf), lambda i, f: (0, f)),
                  pl.BlockSpec((tf, d), lambda i, f: (f, 0))],
        out_specs=pl.BlockSpec((tm, d), lambda i, f: (i, 0)),
        out_shape=jax.ShapeDtypeStruct((rows, d), F32),
        scratch_shapes=[pltpu.VMEM((tm, d), BF16)],
        compiler_params=_params(2),
        name="mlp",
    )(x, g.reshape(1, d), wu_bf, wd_bf)


def _mlp_cast_kernel(x_ref, g_ref, wu_ref, wd_ref, o_ref, wub_ref, wdb_ref, xn_ref):
    @pl.when(pl.program_id(0) == 0)
    def _():
        xv = x_ref[...]
        xn_ref[...] = _rms(xv, g_ref[...]).astype(BF16)
        o_ref[...] = xv

    wu = wu_ref[0].astype(BF16)
    wd = wd_ref[0].astype(BF16)
    wub_ref[...] = wu
    wdb_ref[...] = wd
    h = jnp.dot(xn_ref[...], wu, preferred_element_type=F32)
    h = jnp.maximum(h, 0.0)
    o_ref[...] += jnp.dot((h * h).astype(BF16), wd, preferred_element_type=F32)


def _mlp_cast(x, g, w_up, w_down, layer, tf=512):
    rows, d = x.shape
    dff = w_up.shape[2]
    return pl.pallas_call(
        _mlp_cast_kernel,
        grid=(dff // tf,),
        in_specs=[pl.BlockSpec((rows, d), lambda f: (0, 0)),
                  pl.BlockSpec((1, d), lambda f: (0, 0)),
                  pl.BlockSpec((1, d, tf), lambda f: (layer, 0, f)),
                  pl.BlockSpec((1, tf, d), lambda f: (layer, f, 0))],
        out_specs=[pl.BlockSpec((rows, d), lambda f: (0, 0)),
                   pl.BlockSpec((d, tf), lambda f: (0, f)),
                   pl.BlockSpec((tf, d), lambda f: (f, 0))],
        out_shape=[jax.ShapeDtypeStruct((rows, d), F32),
                   jax.ShapeDtypeStruct((d, dff), BF16),
                   jax.ShapeDtypeStruct((dff, d), BF16)],
        scratch_shapes=[pltpu.VMEM((rows, d), BF16)],
        compiler_params=_params(1),
        name="mlp_cast",
    )(x, g.reshape(1, d), w_up, w_down)


def _attn_prompt_kernel(q_ref, k_ref, v_ref, o_ref, *, scale):
    q = q_ref[...]
    k = k_ref[0].astype(BF16)
    s = lax.dot_general(q, k, (((1,), (1,)), ((), ())), preferred_element_type=F32) * scale
    m = jnp.max(s, axis=-1, keepdims=True)
    e = jnp.exp(s - m)
    p = e / jnp.sum(e, axis=-1, keepdims=True)
    o = jnp.dot(p.astype(BF16), v_ref[0].astype(BF16), preferred_element_type=F32)
    o_ref[...] = o.astype(o_ref.dtype)


def _attn_prompt(q, k, v, n_seq, seq, heads, tq=1024):
    rows, d = q.shape
    m = k.shape[1]
    hd = d // heads
    tq = min(tq, seq)
    tps = seq // tq
    kern = functools.partial(_attn_prompt_kernel, scale=hd ** -0.5)
    return pl.pallas_call(
        kern,
        grid=(n_seq, tps, heads),
        in_specs=[pl.BlockSpec((tq, hd), lambda b, i, h: (b * tps + i, h)),
                  pl.BlockSpec((1, m, hd), lambda b, i, h: (b, 0, h)),
                  pl.BlockSpec((1, m, hd), lambda b, i, h: (b, 0, h))],
        out_specs=pl.BlockSpec((tq, hd), lambda b, i, h: (b * tps + i, h)),
        out_shape=jax.ShapeDtypeStruct((rows, d), BF16),
        compiler_params=_params(3),
        name="attn_prompt",
    )(q, k, v)


def _attn_sample_kernel(q_ref, k_ref, v_ref, o_ref, *, scale, bt):
    for b in range(bt):
        q = q_ref[b]
        s = jnp.sum(k_ref[0, b] * q[None], axis=-1, keepdims=True) * scale
        mx = jnp.max(s, axis=0, keepdims=True)
        e = jnp.exp(s - mx)
        p = e / jnp.sum(e, axis=0, keepdims=True)
        o_ref[b] = jnp.sum(p * v_ref[0, b], axis=0)


def _attn_sample(q, cache_k, cache_v, layer, bt=2):
    bsz, heads, hd = q.shape
    m = cache_k.shape[2]
    kern = functools.partial(_attn_sample_kernel, scale=hd ** -0.5, bt=bt)
    kv_spec = pl.BlockSpec((1, bt, m, heads, hd), lambda i: (layer, i, 0, 0, 0))
    return pl.pallas_call(
        kern,
        grid=(bsz // bt,),
        in_specs=[pl.BlockSpec((bt, heads, hd), lambda i: (i, 0, 0)), kv_spec, kv_spec],
        out_specs=pl.BlockSpec((bt, heads, hd), lambda i: (i, 0, 0)),
        out_shape=jax.ShapeDtypeStruct((bsz, heads, hd), F32),
        compiler_params=_params(1),
        name="attn_sample",
    )(q, cache_k, cache_v)


def _lru_gates(xc, wa, ba, wi, bi, lam):
    xcb = xc.astype(BF16)
    r = jax.nn.sigmoid(jnp.dot(xcb, wa.astype(BF16), preferred_element_type=F32) + ba)
    ig = jax.nn.sigmoid(jnp.dot(xcb, wi.astype(BF16), preferred_element_type=F32) + bi)
    log_a = -LRU_C * r * jax.nn.softplus(-lam)
    a = jnp.exp(log_a)
    u = jnp.sqrt(-jnp.tanh(log_a) * (a * a + 1.0)) * (ig * xc)
    return a, u


def _lru_prompt_kernel(xb_ref, gt_ref, cw_ref, cb_ref, wa_ref, ba_ref, wi_ref, bi_ref, lam_ref,
                       y_ref, cst_ref, hl_ref, ext_ref, a_ref, u_ref, hs_ref, hc_ref,
                       *, tiles_per_seq, taps):
    tm, cb = xb_ref.shape
    halo = SUBLANES

    @pl.when(pl.program_id(1) % tiles_per_seq == 0)
    def _():
        ext_ref[0:halo, :] = jnp.zeros((halo, cb), F32)
        hc_ref[...] = jnp.zeros((1, cb), F32)

    ext_ref[halo:halo + tm, :] = xb_ref[...]
    cw = cw_ref[...]
    xc = cb_ref[...]
    for k in range(taps):
        start = halo - (taps - 1) + k
        xc = xc + cw[k:k + 1, :] * ext_ref[start:start + tm, :]
    cst_ref[0] = ext_ref[halo + tm - (taps - 1):halo + tm, :]
    ext_ref[0:halo, :] = ext_ref[tm:tm + halo, :]

    a, u = _lru_gates(xc, wa_ref[0], ba_ref[...], wi_ref[0], bi_ref[...], lam_ref[...])

    groups = tm // SUBLANES
    a = a.reshape(groups, SUBLANES, cb)
    u = u.reshape(groups, SUBLANES, cb)
    row = lax.broadcasted_iota(jnp.int32, (groups, SUBLANES, cb), 1)
    s = 1
    while s < SUBLANES:
        keep = row >= s
        u = u + a * jnp.where(keep, pltpu.roll(u, s, 1), 0.0)
        a = a * jnp.where(keep, pltpu.roll(a, s, 1), 1.0)
        s *= 2
    a_ref[...] = a.reshape(tm, cb)
    u_ref[...] = u.reshape(tm, cb)

    def link(g, h):
        off = pl.multiple_of(g * SUBLANES, SUBLANES)
        ag = a_ref[pl.ds(off, SUBLANES), :]
        ug = u_ref[pl.ds(off, SUBLANES), :]
        hs_ref[pl.ds(off, SUBLANES), :] = ag * h + ug
        return ag[SUBLANES - 1:SUBLANES, :] * h + ug[SUBLANES - 1:SUBLANES, :]

    h_last = lax.fori_loop(0, tm // SUBLANES, link, hc_ref[...], unroll=4)
    hc_ref[...] = h_last
    hl_ref[0] = h_last
    y_ref[...] = (hs_ref[...] * jax.nn.gelu(gt_ref[...])).astype(y_ref.dtype)


def _lru_prompt(proj, conv_w, conv_b, w_a, b_a, w_i, b_i, lam, n_seq, seq, tm=1024):
    rows, r2 = proj.shape
    r = r2 // 2
    nb, cb = w_a.shape[0], w_a.shape[1]
    taps = conv_w.shape[0]
    tm = min(tm, seq)
    tps = seq // tm
    kern = functools.partial(_lru_prompt_kernel, tiles_per_seq=tps, taps=taps)
    vec = lambda c, i: (0, c)
    y, cst, hl = pl.pallas_call(
        kern,
        grid=(nb, rows // tm),
        in_specs=[pl.BlockSpec((tm, cb), lambda c, i: (i, c)),
                  pl.BlockSpec((tm, cb), lambda c, i: (i, nb + c)),
                  pl.BlockSpec((taps, cb), vec),
                  pl.BlockSpec((1, cb), vec),
                  pl.BlockSpec((1, cb, cb), lambda c, i: (c, 0, 0)),
                  pl.BlockSpec((1, cb), vec),
                  pl.BlockSpec((1, cb, cb), lambda c, i: (c, 0, 0)),
                  pl.BlockSpec((1, cb), vec),
                  pl.BlockSpec((1, cb), vec)],
        out_specs=[pl.BlockSpec((tm, cb), lambda c, i: (i, c)),
                   pl.BlockSpec((1, taps - 1, cb), lambda c, i: (i // tps, 0, c)),
                   pl.BlockSpec((1, 1, cb), lambda c, i: (i // tps, 0, c))],
        out_shape=[jax.ShapeDtypeStruct((rows, r), BF16),
                   jax.ShapeDtypeStruct((n_seq, taps - 1, r), F32),
                   jax.ShapeDtypeStruct((n_seq, 1, r), F32)],
        scratch_shapes=[pltpu.VMEM((tm + SUBLANES, cb), F32),
                        pltpu.VMEM((tm, cb), F32),
                        pltpu.VMEM((tm, cb), F32),
                        pltpu.VMEM((tm, cb), F32),
                        pltpu.VMEM((1, cb), F32)],
        compiler_params=_params(2),
        name="lru_prompt",
    )(proj, proj, conv_w, conv_b.reshape(1, r), w_a, b_a.reshape(1, r), w_i, b_i.reshape(1, r),
      lam.reshape(1, r))
    return y, cst, hl.reshape(n_seq, r)


def _lru_sample_kernel(xb_ref, gt_ref, c0_ref, c1_ref, c2_ref, h0_ref, cw_ref, cb_ref,
                       wa_ref, ba_ref, wi_ref, bi_ref, lam_ref, y_ref, h_ref):
    cw = cw_ref[...]
    xc = cb_ref[...]
    for k, c_ref in enumerate((c0_ref, c1_ref, c2_ref, xb_ref)):
        xc = xc + cw[k:k + 1, :] * c_ref[...]
    a, u = _lru_gates(xc, wa_ref[0], ba_ref[...], wi_ref[0], bi_ref[...], lam_ref[...])
    h = a * h0_ref[...] + u
    h_ref[...] = h
    y_ref[...] = (h * jax.nn.gelu(gt_ref[...])).astype(y_ref.dtype)


def _lru_sample(proj, conv_rows, h0, conv_w, conv_b, w_a, b_a, w_i, b_i, lam):
    bsz, r2 = proj.shape
    r = r2 // 2
    nb, cb = w_a.shape[0], w_a.shape[1]
    taps = conv_w.shape[0]
    blk = lambda c: (0, c)
    return pl.pallas_call(
        _lru_sample_kernel,
        grid=(nb,),
        in_specs=[pl.BlockSpec((bsz, cb), blk),
                  pl.BlockSpec((bsz, cb), lambda c: (0, nb + c)),
                  pl.BlockSpec((bsz, cb), blk), pl.BlockSpec((bsz, cb), blk),
                  pl.BlockSpec((bsz, cb), blk), pl.BlockSpec((bsz, cb), blk),
                  pl.BlockSpec((taps, cb), blk), pl.BlockSpec((1, cb), blk),
                  pl.BlockSpec((1, cb, cb), lambda c: (c, 0, 0)), pl.BlockSpec((1, cb), blk),
                  pl.BlockSpec((1, cb, cb), lambda c: (c, 0, 0)), pl.BlockSpec((1, cb), blk),
                  pl.BlockSpec((1, cb), blk)],
        out_specs=[pl.BlockSpec((bsz, cb), blk), pl.BlockSpec((bsz, cb), blk)],
        out_shape=[jax.ShapeDtypeStruct((bsz, r), BF16), jax.ShapeDtypeStruct((bsz, r), F32)],
        compiler_params=_params(1),
        name="lru_sample",
    )(proj, proj, conv_rows[0], conv_rows[1], conv_rows[2], h0, conv_w, conv_b.reshape(1, r),
      w_a, b_a.reshape(1, r), w_i, b_i.reshape(1, r), lam.reshape(1, r))


def _pool_prompt_kernel(xn_ref, x_ref, w_ref, b_ref, sc_ref, o_ref, st_ref, ext_ref, wbf_ref,
                        *, tiles_per_seq, windows):
    tm, d = xn_ref.shape
    gd = d // len(windows)
    halo = 2 * SUBLANES
    i = pl.program_id(0)

    @pl.when(i == 0)
    def _():
        wbf_ref[...] = w_ref[...].astype(BF16)

    @pl.when(i % tiles_per_seq == 0)
    def _():
        ext_ref[0:halo, :] = jnp.zeros((halo, d), F32)

    ext_ref[halo:halo + tm, :] = xn_ref[...]
    pos = (i % tiles_per_seq) * tm + lax.broadcasted_iota(jnp.int32, (tm, 1), 0)
    for g, w in enumerate(windows):
        sl = slice(g * gd, (g + 1) * gd)
        s = ext_ref[:, sl]
        shift = 1
        while shift < w:
            s = s + pltpu.roll(s, shift, 0)
            shift *= 2
        cnt = jnp.minimum(pos + 1, w).astype(F32)
        pooled = s[halo:, :] / cnt
        diff = (pooled - xn_ref[:, sl]).astype(BF16)
        mixed = jnp.dot(diff, wbf_ref[g], preferred_element_type=F32) + b_ref[:, sl]
        o_ref[:, sl] = x_ref[:, sl] + mixed * sc_ref[:, sl]
    st_ref[0] = ext_ref[halo + tm - POOL_BUF:halo + tm, :]
    ext_ref[0:halo, :] = ext_ref[tm:tm + halo, :]


def _pool_prompt(xn, x, w, b, scale, n_seq, seq, tm=512):
    rows, d = xn.shape
    tm = min(tm, seq)
    tps = seq // tm
    ng, gd = w.shape[0], w.shape[1]
    kern = functools.partial(_pool_prompt_kernel, tiles_per_seq=tps, windows=POOL_WINDOWS)
    return pl.pallas_call(
        kern,
        grid=(rows // tm,),
        in_specs=[pl.BlockSpec((tm, d), lambda i: (i, 0)),
                  pl.BlockSpec((tm, d), lambda i: (i, 0)),
                  pl.BlockSpec((ng, gd, gd), lambda i: (0, 0, 0)),
                  pl.BlockSpec((1, d), lambda i: (0, 0)),
                  pl.BlockSpec((1, d), lambda i: (0, 0))],
        out_specs=[pl.BlockSpec((tm, d), lambda i: (i, 0)),
                   pl.BlockSpec((1, POOL_BUF, d), lambda i: (i // tps, 0, 0))],
        out_shape=[jax.ShapeDtypeStruct((rows, d), F32),
                   jax.ShapeDtypeStruct((n_seq, POOL_BUF, d), F32)],
        scratch_shapes=[pltpu.VMEM((tm + 2 * SUBLANES, d), F32),
                        pltpu.VMEM((ng, gd, gd), BF16)],
        compiler_params=_params(1),
        name="pool_prompt",
    )(xn, x, w, b.reshape(1, d), scale.reshape(1, d))


def _pool_sample_kernel(xn_ref, x_ref, st_ref, w_ref, b_ref, sc_ref, o_ref, *, window, pos0):
    xn = xn_ref[...]
    s = xn
    for r in range(POOL_BUF - (window - 1), POOL_BUF):
        s = s + st_ref[r]
    pooled = s / float(min(pos0 + 1, window))
    diff = (pooled - xn).astype(BF16)
    mixed = jnp.dot(diff, w_ref[0].astype(BF16), preferred_element_type=F32) + b_ref[...]
    o_ref[...] = x_ref[...] + mixed * sc_ref[...]


def _pool_sample(xn, x, state_t, w, b, scale, pos0):
    bsz, d = xn.shape
    ng, gd = w.shape[0], w.shape[1]
    outs = []
    for g, window in enumerate(POOL_WINDOWS):
        kern = functools.partial(_pool_sample_kernel, window=window, pos0=pos0)
        blk = lambda i, g=g: (0, g)
        outs.append(pl.pallas_call(
            kern,
            grid=(1,),
            in_specs=[pl.BlockSpec((bsz, gd), blk), pl.BlockSpec((bsz, gd), blk),
                      pl.BlockSpec((POOL_BUF, bsz, gd), lambda i, g=g: (0, 0, g)),
                      pl.BlockSpec((1, gd, gd), lambda i, g=g: (g, 0, 0)),
                      pl.BlockSpec((1, gd), blk), pl.BlockSpec((1, gd), blk)],
            out_specs=pl.BlockSpec((bsz, gd), lambda i: (0, 0)),
            out_shape=jax.ShapeDtypeStruct((bsz, gd), F32),
            compiler_params=_params(1),
            name="pool_sample",
        )(xn, x, state_t, w, b.reshape(1, d), scale.reshape(1, d)))
    return jnp.concatenate(outs, axis=-1)


def _s5_matrices(a_re, a_im, b_re, b_im, c_re, c_im, log_dt, chunk):
    g_all, p = a_re.shape
    c = b_re.shape[-1]
    gpb = S5_GROUPS_PER_BLOCK
    nb = g_all // gpb
    dt = jnp.exp(log_dt.astype(F32))[:, None]
    mag = jnp.exp(a_re * dt)
    lr, li = mag * jnp.cos(a_im * dt), mag * jnp.sin(a_im * dt)
    den = a_re * a_re + a_im * a_im
    q_re = ((lr - 1.0) * a_re + li * a_im) / den
    q_im = (li * a_re - (lr - 1.0) * a_im) / den
    bb_re = q_re[..., None] * b_re - q_im[..., None] * b_im
    bb_im = q_re[..., None] * b_im + q_im[..., None] * b_re

    def cmul(xr, xi, yr, yi):
        return xr * yr - xi * yi, xr * yi + xi * yr

    pw = [(jnp.ones_like(lr), jnp.zeros_like(lr))]
    for _ in range(chunk):
        pw.append(cmul(pw[-1][0], pw[-1][1], lr, li))
    hi = lax.Precision.HIGHEST
    lane_g = jnp.arange(LANES) // c
    state_g = jnp.arange(gpb * p) // p
    mask_cc = (lane_g[:, None] == lane_g[None, :]).astype(F32)
    mask_cp = (lane_g[:, None] == state_g[None, :]).astype(F32)
    spread = jnp.tile(jnp.eye(p, dtype=F32), (1, gpb))

    def lanes_by_state(x):
        return x.transpose(0, 2, 1).reshape(nb, LANES, p)

    def state_by_lanes(x):
        return x.transpose(2, 0, 1).reshape(p, nb, LANES).transpose(1, 0, 2)

    def c_lam(tau):
        return cmul(c_re, c_im, pw[tau][0][:, None, :], pw[tau][1][:, None, :])

    bb_rows = jnp.concatenate([lanes_by_state(bb_re), -lanes_by_state(bb_im)], axis=-1)
    blocks = []
    for tau in range(chunk):
        wr, wi = c_lam(tau)
        w_cols = jnp.concatenate([state_by_lanes(wr), state_by_lanes(wi)], axis=1)
        blocks.append(mask_cc * jnp.einsum('brk,bkc->brc', bb_rows, w_cols, precision=hi))
    zero = jnp.zeros_like(blocks[0])
    m_in = jnp.stack([jnp.concatenate([blocks[t - s] if t >= s else zero for t in range(chunk)],
                                      axis=-1) for s in range(chunk)], axis=1)
    m_in = m_in.reshape(nb, chunk * LANES, chunk * LANES).astype(BF16)

    rows_t = []
    for t in range(chunk):
        pr, pi = pw[chunk - 1 - t]
        er, ei = cmul(pr[..., None], pi[..., None], bb_re, bb_im)
        halves = [mask_cp * jnp.einsum('brp,pq->brq', lanes_by_state(e), spread, precision=hi)
                  for e in (er, ei)]
        rows_t.append(jnp.concatenate(halves, axis=-1))
    m_state = jnp.stack(rows_t, axis=1).reshape(nb, chunk * LANES, 2 * gpb * p).astype(BF16)

    cols_t = []
    for t in range(chunk):
        wr, wi = c_lam(t + 1)
        halves = [mask_cp.T * jnp.einsum('qp,bpc->bqc', spread.T, state_by_lanes(w), precision=hi)
                  for w in (wr, -wi)]
        cols_t.append(jnp.concatenate(halves, axis=1))
    m_out = jnp.concatenate(cols_t, axis=-1).astype(BF16)

    qr, qi = pw[chunk]
    rows = [(jnp.ones_like(qr), jnp.zeros_like(qr))]
    for _ in range(SUBLANES):
        rows.append(cmul(rows[-1][0], rows[-1][1], qr, qi))
    table = jnp.stack([_s5_state_layout(r_, i_) for r_, i_ in rows])
    table = jnp.concatenate([table, jnp.zeros((2 * SUBLANES - table.shape[0], table.shape[1]), F32)])
    return m_in, m_state, m_out, table


def _s5_state_layout(re, im):
    lead = re.shape[:-2]
    g_all, p = re.shape[-2:]
    nb = g_all // S5_GROUPS_PER_BLOCK
    st = jnp.stack([re.reshape(lead + (nb, S5_GROUPS_PER_BLOCK * p)),
                    im.reshape(lead + (nb, S5_GROUPS_PER_BLOCK * p))], axis=-2)
    return st.reshape(lead + (2 * g_all * p,))


def _s5_state_unlayout(flat, g_all, p):
    lead = flat.shape[:-1]
    nb = g_all // S5_GROUPS_PER_BLOCK
    st = flat.reshape(lead + (nb, 2, S5_GROUPS_PER_BLOCK * p))
    return st[..., 0, :].reshape(lead + (g_all, p)), st[..., 1, :].reshape(lead + (g_all, p))


def _s5_lhs(u_refs):
    return jnp.concatenate([u[...] for u in u_refs], axis=1).astype(BF16)


def _s5_state_in_kernel(*refs, chunk):
    u_refs, m_ref, e_ref = refs[:chunk], refs[chunk], refs[chunk + 1]
    e_ref[...] = jnp.dot(_s5_lhs(u_refs), m_ref[0], preferred_element_type=F32)


def _s5_state_in(xnv, m_state, chunk, d, tr=1024):
    r = xnv.shape[0]
    nb, _, sw = m_state.shape
    tr = min(tr, r)
    lb = d // LANES
    kern = functools.partial(_s5_state_in_kernel, chunk=chunk)
    return pl.pallas_call(
        kern,
        grid=(nb, r // tr),
        in_specs=[pl.BlockSpec((tr, LANES), lambda b, i, t=t: (i, t * lb + b)) for t in range(chunk)]
        + [pl.BlockSpec((1, chunk * LANES, sw), lambda b, i: (b, 0, 0))],
        out_specs=pl.BlockSpec((tr, sw), lambda b, i: (i, b)),
        out_shape=jax.ShapeDtypeStruct((r, nb * sw), F32),
        compiler_params=_params(2),
        name="s5_state_in",
    )(*([xnv] * chunk), m_state)


def _s5_scan_kernel(e_ref, tab_ref, h_ref, fin_ref):
    rps, sw = e_ref.shape
    half = sw // 2
    er, ei = e_ref[:, :half], e_ref[:, half:]
    row = lax.broadcasted_iota(jnp.int32, (rps, half), 0) % SUBLANES
    s = 1
    while s < SUBLANES:
        keep = row >= s
        pr, pi = tab_ref[s:s + 1, :half], tab_ref[s:s + 1, half:]
        sr = jnp.where(keep, pltpu.roll(er, s, 0), 0.0)
        si = jnp.where(keep, pltpu.roll(ei, s, 0), 0.0)
        er, ei = er + pr * sr - pi * si, ei + pr * si + pi * sr
        s *= 2
    xr = jnp.where(row >= 1, pltpu.roll(er, 1, 0), 0.0)
    xi = jnp.where(row >= 1, pltpu.roll(ei, 1, 0), 0.0)
    ar, ai = tab_ref[0:SUBLANES, :half], tab_ref[0:SUBLANES, half:]
    l8r, l8i = tab_ref[SUBLANES:SUBLANES + 1, :half], tab_ref[SUBLANES:SUBLANES + 1, half:]
    cr = jnp.zeros((1, half), F32)
    ci = jnp.zeros((1, half), F32)
    for g in range(rps // SUBLANES):
        lo, hi = g * SUBLANES, (g + 1) * SUBLANES
        h_ref[lo:hi, :half] = ar * cr - ai * ci + xr[lo:hi]
        h_ref[lo:hi, half:] = ar * ci + ai * cr + xi[lo:hi]
        cr, ci = (l8r * cr - l8i * ci + er[hi - 1:hi], l8r * ci + l8i * cr + ei[hi - 1:hi])
    fin_ref[0, :, :half] = cr
    fin_ref[0, :, half:] = ci


def _s5_scan(e, table, n_seq, sw):
    r, width = e.shape
    rps = r // n_seq
    return pl.pallas_call(
        _s5_scan_kernel,
        grid=(width // sw, n_seq),
        in_specs=[pl.BlockSpec((rps, sw), lambda c, b: (b, c)),
                  pl.BlockSpec((2 * SUBLANES, sw), lambda c, b: (0, c))],
        out_specs=[pl.BlockSpec((rps, sw), lambda c, b: (b, c)),
                   pl.BlockSpec((1, 1, sw), lambda c, b: (b, 0, c))],
        out_shape=[jax.ShapeDtypeStruct((r, width), F32),
                   jax.ShapeDtypeStruct((n_seq, 1, width), F32)],
        compiler_params=_params(2),
        name="s5_scan",
    )(e, table)


def _s5_step_kernel(e_ref, h0_ref, tab_ref, o_ref):
    half = e_ref.shape[1] // 2
    lr, li = tab_ref[1:2, :half], tab_ref[1:2, half:]
    hr, hi = h0_ref[:, :half], h0_ref[:, half:]
    o_ref[:, :half] = lr * hr - li * hi + e_ref[:, :half]
    o_ref[:, half:] = lr * hi + li * hr + e_ref[:, half:]


def _s5_step(e, h0, table, sw):
    bsz, width = e.shape
    return pl.pallas_call(
        _s5_step_kernel,
        grid=(width // sw,),
        in_specs=[pl.BlockSpec((bsz, sw), lambda c: (0, c)),
                  pl.BlockSpec((bsz, sw), lambda c: (0, c)),
                  pl.BlockSpec((2 * SUBLANES, sw), lambda c: (0, c))],
        out_specs=pl.BlockSpec((bsz, sw), lambda c: (0, c)),
        out_shape=jax.ShapeDtypeStruct((bsz, width), F32),
        compiler_params=_params(1),
        name="s5_step",
    )(e, h0, table)


def _s5_out_kernel(*refs, chunk):
    u_refs = refs[:chunk]
    h_ref, mi_ref, mo_ref = refs[chunk:chunk + 3]
    y_refs = refs[chunk + 3:]
    y = jnp.dot(_s5_lhs(u_refs), mi_ref[0], preferred_element_type=F32)
    y = y + jnp.dot(h_ref[...].astype(BF16), mo_ref[0], preferred_element_type=F32)
    for t, y_ref in enumerate(y_refs):
        y_ref[...] = y[:, t * LANES:(t + 1) * LANES]


def _s5_out(xnv, h, m_in, m_out, chunk, d, tr=1024):
    r = xnv.shape[0]
    nb, sw, _ = m_out.shape
    tr = min(tr, r)
    lb = d // LANES
    kern = functools.partial(_s5_out_kernel, chunk=chunk)
    return pl.pallas_call(
        kern,
        grid=(nb, r // tr),
        in_specs=[pl.BlockSpec((tr, LANES), lambda b, i, t=t: (i, t * lb + b)) for t in range(chunk)]
        + [pl.BlockSpec((tr, sw), lambda b, i: (i, b)),
           pl.BlockSpec((1, chunk * LANES, chunk * LANES), lambda b, i: (b, 0, 0)),
           pl.BlockSpec((1, sw, chunk * LANES), lambda b, i: (b, 0, 0))],
        out_specs=[pl.BlockSpec((tr, LANES), lambda b, i: (i, b)) for _ in range(chunk)],
        out_shape=[jax.ShapeDtypeStruct((r, d), F32) for _ in range(chunk)],
        compiler_params=_params(2),
        name="s5_out",
    )(*([xnv] * chunk), h, m_in, m_out)


def _s5_gelu_kernel(y_ref, xn_ref, d_ref, z_ref):
    z_ref[...] = jax.nn.gelu(y_ref[...] + d_ref[...] * xn_ref[...]).astype(z_ref.dtype)


def _s5_gelu(y, xn, dvec, tm=512):
    rows, d = xn.shape
    tm = min(tm, rows)
    blk = pl.BlockSpec((tm, d), lambda i: (i, 0))
    return pl.pallas_call(
        _s5_gelu_kernel,
        grid=(rows // tm,),
        in_specs=[blk, blk, pl.BlockSpec((1, d), lambda i: (0, 0))],
        out_specs=blk,
        out_shape=jax.ShapeDtypeStruct((rows, d), BF16),
        compiler_params=_params(1),
        name="s5_gelu",
    )(y, xn, dvec.reshape(1, d))


def _s5_mixer(x, xn, h0_flat, n_seq, prm, w_glu, b_glu, layer, chunk):
    a_re, a_im, b_re, b_im, c_re, c_im, dvec, log_dt = prm
    rows, d = xn.shape
    m_in, m_state, m_out, table = _s5_matrices(a_re, a_im, b_re, b_im, c_re, c_im, log_dt, chunk)
    sw = m_state.shape[-1]
    xnv = xn.reshape(rows // chunk, chunk * d)
    e = _s5_state_in(xnv, m_state, chunk, d)
    if h0_flat is None:
        h, fin = _s5_scan(e, table, n_seq, sw)
        fin = fin.reshape(n_seq, -1)
    else:
        h = h0_flat
        fin = _s5_step(e, h0_flat, table, sw)
    ys = _s5_out(xnv, h, m_in, m_out, chunk, d)
    y = jnp.stack(ys, axis=1).reshape(rows, d)
    z = _s5_gelu(y, xn, dvec)
    b_glu3 = b_glu.reshape(b_glu.shape[0], 1, 2 * d)
    tiles = dict(tm=512, tn=1024, single_w=True) if h0_flat is None else {}
    x_new = _mm(z, [(w_glu, layer, 0), (w_glu, layer, d)], d, _ep_glu_residual, F32,
                rows_extra=[(b_glu3, layer, 0), (b_glu3, layer, d)],
                full_extra=[x], name="s5_glu", **tiles)
    return x_new, fin


def kernel(x_prompt, x_sample, cache_mem_k, cache_mem_v, state_lru_conv, state_lru_h, state_pool, state_s5_re, state_s5_im, mem_prompt, g_mix, g_xattn, g_mem, g_mlp, g_final, w_q, w_k, w_v, w_o, w_up, w_down, lru_w_in, lru_conv_w, lru_conv_b, lru_w_a, lru_b_a, lru_w_i, lru_b_i, lru_lambda, lru_w_o, pool_w, pool_b, pool_scale, s5_a_re, s5_a_im, s5_b_re, s5_b_im, s5_c_re, s5_c_im, s5_d, s5_log_dt, s5_w_glu, s5_b_glu):
    n_seq, seq, d = x_prompt.shape
    bsz, dec_seq, _ = x_sample.shape
    assert dec_seq == 1, "the sample group advances one token per request"
    depth = g_mix.shape[0]
    mem_tokens = mem_prompt.shape[1]
    heads, head_dim = cache_mem_k.shape[3], cache_mem_k.shape[4]
    g_all, p_state = s5_a_re.shape[1], s5_a_re.shape[2]
    past_len = 16384

    xp = x_prompt.reshape(n_seq * seq, d)
    xs = x_sample.reshape(bsz, d)
    mem = mem_prompt.reshape(n_seq * mem_tokens, d)
    wide = dict(tm=512, tn=d, single_w=True)

    mem_k, mem_v = [], []
    lru_conv_p, lru_h_p, lru_conv_s, lru_h_s = [], [], [], []
    pool_p, pool_s = [], []
    s5_re_p, s5_im_p, s5_re_s, s5_im_s = [], [], [], []

    for i in range(depth):
        kind, j = i % 3, i // 3
        if kind == 0:
            prm = (lru_conv_w[j], lru_conv_b[j], lru_w_a[j], lru_b_a[j], lru_w_i[j], lru_b_i[j],
                   lru_lambda[j])
            proj = _mm(xp, [(lru_w_in, j, 0)], 2 * d, _ep_plain, F32, norm_g=g_mix[i],
                       name="lru_in", **wide)
            y, cst, hl = _lru_prompt(proj, *prm, n_seq, seq)
            xp = _mm(y, [(lru_w_o, j, 0)], d, _ep_residual, F32, full_extra=[xp], name="lru_out",
                     **wide)
            lru_conv_p.append(cst)
            lru_h_p.append(hl)

            proj = _mm(xs, [(lru_w_in, j, 0)], 2 * d, _ep_plain, F32, norm_g=g_mix[i],
                       name="lru_in_s")
            conv_rows = [state_lru_conv[j][:, r, :] for r in range(state_lru_conv.shape[2])]
            y, h_new = _lru_sample(proj, conv_rows, state_lru_h[j], *prm)
            xs = _mm(y, [(lru_w_o, j, 0)], d, _ep_residual, F32, full_extra=[xs], name="lru_out_s")
            lru_conv_s.append(jnp.concatenate(
                [state_lru_conv[j][:, 1:, :], proj[:, None, :d]], axis=1))
            lru_h_s.append(h_new)
        elif kind == 1:
            xn = _rmsnorm(xp, g_mix[i], F32)
            xp, st = _pool_prompt(xn, xp, pool_w[j], pool_b[j], pool_scale[j], n_seq, seq)
            pool_p.append(st)

            xn = _rmsnorm(xs, g_mix[i], F32)
            xs = _pool_sample(xn, xs, jnp.swapaxes(state_pool[j], 0, 1), pool_w[j], pool_b[j],
                              pool_scale[j], past_len)
            pool_s.append(jnp.concatenate([state_pool[j][:, 1:, :], xn[:, None, :]], axis=1))
        else:
            prm = (s5_a_re[j], s5_a_im[j], s5_b_re[j], s5_b_im[j], s5_c_re[j], s5_c_im[j],
                   s5_d[j], s5_log_dt[j])
            xp, fin = _s5_mixer(xp, _rmsnorm(xp, g_mix[i], F32), None, n_seq, prm,
                                s5_w_glu, s5_b_glu, j, S5_CHUNK)
            re, im = _s5_state_unlayout(fin, g_all, p_state)
            s5_re_p.append(re)
            s5_im_p.append(im)

            h0 = _s5_state_layout(state_s5_re[j], state_s5_im[j])
            xs, fin = _s5_mixer(xs, _rmsnorm(xs, g_mix[i], F32), h0, bsz, prm,
                                s5_w_glu, s5_b_glu, j, 1)
            re, im = _s5_state_unlayout(fin, g_all, p_state)
            s5_re_s.append(re)
            s5_im_s.append(im)

        k = _mm(mem, [(w_k, i, 0)], d, _ep_plain, F32, norm_g=g_mem[i], name="mem_k")
        v = _mm(mem, [(w_v, i, 0)], d, _ep_plain, F32, norm_g=g_mem[i], name="mem_v")
        mem_k.append(k.reshape(n_seq, mem_tokens, heads, head_dim))
        mem_v.append(v.reshape(n_seq, mem_tokens, heads, head_dim))
        q = _mm(xp, [(w_q, i, 0)], d, _ep_plain, BF16, norm_g=g_xattn[i], name="q", **wide)
        o = _attn_prompt(q, k.reshape(n_seq, mem_tokens, d), v.reshape(n_seq, mem_tokens, d),
                         n_seq, seq, heads)
        xp = _mm(o, [(w_o, i, 0)], d, _ep_residual, F32, full_extra=[xp], name="attn_out", **wide)

        q = _mm(xs, [(w_q, i, 0)], d, _ep_plain, F32, norm_g=g_xattn[i], name="q_s")
        o = _attn_sample(q.reshape(bsz, heads, head_dim), cache_mem_k, cache_mem_v, i)
        xs = _mm(o.reshape(bsz, d), [(w_o, i, 0)], d, _ep_residual, F32, full_extra=[xs],
                 name="attn_out_s")

        xs, wu_bf, wd_bf = _mlp_cast(xs, g_mlp[i], w_up, w_down, i)
        xp = _mlp(xp, g_mlp[i], wu_bf, wd_bf)

    y_prompt = _rmsnorm(xp, g_final, F32).reshape(n_seq, seq, d)
    y_sample = _rmsnorm(xs, g_final, F32).reshape(bsz, 1, d)
    return (y_prompt, y_sample,
            jnp.stack(mem_k), jnp.stack(mem_v),
            jnp.stack(lru_conv_p), jnp.stack(lru_h_p), jnp.stack(pool_p),
            jnp.stack(s5_re_p), jnp.stack(s5_im_p),
            jnp.stack(lru_conv_s), jnp.stack(lru_h_s), jnp.stack(pool_s),
            jnp.stack(s5_re_s), jnp.stack(s5_im_s))
```

```python
import functools
import math

import jax
import jax.numpy as jnp
from jax import lax
from jax.experimental import pallas as pl
from jax.experimental.pallas import tpu as pltpu

F32 = jnp.float32
BF16 = jnp.bfloat16

SUBLANES = 8
LANES = 128
VMEM_LIMIT_BYTES = 56 * 1024 * 1024

RMS_EPS = 1e-6
LRU_C = 8.0
POOL_WINDOWS = (2, 4, 8, 16)
POOL_BUF = max(POOL_WINDOWS) - 1
S5_GROUP_DIM = 16
S5_CHUNK = 8
S5_GROUPS_PER_BLOCK = LANES // S5_GROUP_DIM


def _params(n_axes):
    return pltpu.CompilerParams(dimension_semantics=("arbitrary",) * n_axes,
                                vmem_limit_bytes=VMEM_LIMIT_BYTES)


def _rms(x, g):
    xf = x.astype(F32)
    inv = lax.rsqrt(jnp.mean(xf * xf, axis=-1, keepdims=True) + RMS_EPS)
    return xf * inv * g.astype(F32)


def _rmsnorm_kernel(x_ref, g_ref, o_ref):
    o_ref[...] = _rms(x_ref[...], g_ref[...]).astype(o_ref.dtype)


def _rmsnorm(x, g, out_dtype, tm=512):
    rows, d = x.shape
    tm = min(tm, rows)
    return pl.pallas_call(
        _rmsnorm_kernel,
        grid=(rows // tm,),
        in_specs=[pl.BlockSpec((tm, d), lambda i: (i, 0)),
                  pl.BlockSpec((1, d), lambda i: (0, 0))],
        out_specs=pl.BlockSpec((tm, d), lambda i: (i, 0)),
        out_shape=jax.ShapeDtypeStruct((rows, d), out_dtype),
        compiler_params=_params(1),
        name="rmsnorm",
    )(x, g.reshape(1, d))


def _mm_kernel(*refs, n_w, n_row, n_full, has_norm, n_lhs, lhs_fn, epilogue):
    x_ref = refs[0]
    refs = refs[1:]
    g_ref = refs[0] if has_norm else None
    refs = refs[1:] if has_norm else refs
    lhs_refs = refs[:n_lhs]
    refs = refs[n_lhs:]
    w_refs = refs[:n_w]
    row_refs = refs[n_w:n_w + n_row]
    full_refs = refs[n_w + n_row:n_w + n_row + n_full]
    o_ref = refs[n_w + n_row + n_full]
    wbf_refs = refs[1 + n_w + n_row + n_full:]

    @pl.when(pl.program_id(1) == 0)
    def _():
        for w_ref, wbf_ref in zip(w_refs, wbf_refs):
            wbf_ref[...] = w_ref[0].astype(BF16)

    if has_norm:
        xb = _rms(x_ref[...], g_ref[...]).astype(BF16)
    elif lhs_fn is not None:
        xb = lhs_fn(x_ref[...], *[r[...] for r in lhs_refs]).astype(BF16)
    else:
        xb = x_ref[...].astype(BF16)
    accs = [jnp.dot(xb, wbf_ref[...], preferred_element_type=F32) for wbf_ref in wbf_refs]
    out = epilogue(accs, [r[0] for r in row_refs], [f[...] for f in full_refs])
    o_ref[...] = out.astype(o_ref.dtype)


def _mm(x, ws, n_cols, epilogue, out_dtype, *, norm_g=None, lhs_fn=None, lhs_extra=(),
        rows_extra=(), full_extra=(), tm=1024, tn=512, single_w=False, name="mm"):
    rows, k = x.shape
    tm = min(tm, rows)
    tn = min(tn, n_cols)
    in_specs = [pl.BlockSpec((tm, k), lambda j, i: (i, 0))]
    args = [x]
    if norm_g is not None:
        in_specs.append(pl.BlockSpec((1, k), lambda j, i: (0, 0)))
        args.append(norm_g.reshape(1, k))
    for e in lhs_extra:
        if e.shape[0] == 1:
            in_specs.append(pl.BlockSpec((1, k), lambda j, i: (0, 0)))
        else:
            in_specs.append(pl.BlockSpec((tm, k), lambda j, i: (i, 0)))
        args.append(e)
    w_mode = dict(pipeline_mode=pl.Buffered(1)) if single_w else {}
    for w, layer, off in ws:
        in_specs.append(pl.BlockSpec((1, k, tn), lambda j, i, l=layer, o=off // tn: (l, 0, o + j),
                                     **w_mode))
        args.append(w)
    for v, layer, off in rows_extra:
        in_specs.append(pl.BlockSpec((1, 1, tn), lambda j, i, l=layer, o=off // tn: (l, 0, o + j)))
        args.append(v)
    for f in full_extra:
        in_specs.append(pl.BlockSpec((tm, tn), lambda j, i: (i, j)))
        args.append(f)
    kern = functools.partial(_mm_kernel, n_w=len(ws), n_row=len(rows_extra),
                             n_full=len(full_extra), has_norm=norm_g is not None,
                             n_lhs=len(lhs_extra), lhs_fn=lhs_fn, epilogue=epilogue)
    return pl.pallas_call(
        kern,
        grid=(n_cols // tn, rows // tm),
        in_specs=in_specs,
        out_specs=pl.BlockSpec((tm, tn), lambda j, i: (i, j)),
        out_shape=jax.ShapeDtypeStruct((rows, n_cols), out_dtype),
        scratch_shapes=[pltpu.VMEM((k, tn), BF16) for _ in ws],
        compiler_params=_params(2),
        name=name,
    )(*args)


def _ep_plain(accs, rows, fulls):
    return accs[0]


def _ep_residual(accs, rows, fulls):
    return fulls[0] + accs[0]


def _ep_glu_residual(accs, rows, fulls):
    a = accs[0] + rows[0]
    b = accs[1] + rows[1]
    return fulls[0] + a * jax.nn.sigmoid(b)


def _mlp_kernel(x_ref, g_ref, wu_ref, wd_ref, o_ref, xn_ref):
    @pl.when(pl.program_id(1) == 0)
    def _():
        xv = x_ref[...]
        xn_ref[...] = _rms(xv, g_ref[...]).astype(BF16)
        o_ref[...] = xv

    h = jnp.dot(xn_ref[...], wu_ref[...], preferred_element_type=F32)
    h = jnp.maximum(h, 0.0)
    o_ref[...] += jnp.dot((h * h).astype(BF16), wd_ref[...], preferred_element_type=F32)


def _mlp(x, g, wu_bf, wd_bf, tm=1024, tf=512):
    rows, d = x.shape
    dff = wu_bf.shape[1]
    tm = min(tm, rows)
    return pl.pallas_call(
        _mlp_kernel,
        grid=(rows // tm, dff // tf),
        in_specs=[pl.BlockSpec((tm, d), lambda i, f: (i, 0)),
                  pl.BlockSpec((1, d), lambda i, f: (0, 0)),
                  pl.BlockSpec((d, tf), lambda i, f: (0, f)),
                  pl.BlockSpec((tf, d), lambda i, f: (f, 0))],
        out_specs=pl.BlockSpec((tm, d), lambda i, f: (i, 0)),
        out_shape=jax.ShapeDtypeStruct((rows, d), F32),
        scratch_shapes=[pltpu.VMEM((tm, d), BF16)],
        compiler_params=_params(2),
        name="mlp",
    )(x, g.reshape(1, d), wu_bf, wd_bf)


def _mlp_cast_kernel(x_ref, g_ref, wu_ref, wd_ref, o_ref, wub_ref, wdb_ref, xn_ref):
    @pl.when(pl.program_id(0) == 0)
    def _():
        xv = x_ref[...]
        xn_ref[...] = _rms(xv, g_ref[...]).astype(BF16)
        o_ref[...] = xv

    wu = wu_ref[0].astype(BF16)
    wd = wd_ref[0].astype(BF16)
    wub_ref[...] = wu
    wdb_ref[...] = wd
    h = jnp.dot(xn_ref[...], wu, preferred_element_type=F32)
    h = jnp.maximum(h, 0.0)
    o_ref[...] += jnp.dot((h * h).astype(BF16), wd, preferred_element_type=F32)


def _mlp_cast(x, g, w_up, w_down, layer, tf=512):
    rows, d = x.shape
    dff = w_up.shape[2]
    return pl.pallas_call(
        _mlp_cast_kernel,
        grid=(dff // tf,),
        in_specs=[pl.BlockSpec((rows, d), lambda f: (0, 0)),
                  pl.BlockSpec((1, d), lambda f: (0, 0)),
                  pl.BlockSpec((1, d, tf), lambda f: (layer, 0, f)),
                  pl.BlockSpec((1, tf, d), lambda f: (layer, f, 0))],
        out_specs=[pl.BlockSpec((rows, d), lambda f: (0, 0)),
                   pl.BlockSpec((d, tf), lambda f: (0, f)),
                   pl.BlockSpec((tf, d), lambda f: (f, 0))],
        out_shape=[jax.ShapeDtypeStruct((rows, d), F32),
                   jax.ShapeDtypeStruct((d, dff), BF16),
                   jax.ShapeDtypeStruct((dff, d), BF16)],
        scratch_shapes=[pltpu.VMEM((rows, d), BF16)],
        compiler_params=_params(1),
        name="mlp_cast",
    )(x, g.reshape(1, d), w_up, w_down)


def _attn_prompt_kernel(q_ref, k_ref, v_ref, o_ref, *, scale):
    q = q_ref[...]
    k = k_ref[0].astype(BF16)
    s = lax.dot_general(q, k, (((1,), (1,)), ((), ())), preferred_element_type=F32) * scale
    m = jnp.max(s, axis=-1, keepdims=True)
    e = jnp.exp(s - m)
    p = e / jnp.sum(e, axis=-1, keepdims=True)
    o = jnp.dot(p.astype(BF16), v_ref[0].astype(BF16), preferred_element_type=F32)
    o_ref[...] = o.astype(o_ref.dtype)


def _attn_prompt(q, k, v, n_seq, seq, heads, tq=1024):
    rows, d = q.shape
    m = k.shape[1]
    hd = d // heads
    tq = min(tq, seq)
    tps = seq // tq
    kern = functools.partial(_attn_prompt_kernel, scale=hd ** -0.5)
    return pl.pallas_call(
        kern,
        grid=(n_seq, tps, heads),
        in_specs=[pl.BlockSpec((tq, hd), lambda b, i, h: (b * tps + i, h)),
                  pl.BlockSpec((1, m, hd), lambda b, i, h: (b, 0, h)),
                  pl.BlockSpec((1, m, hd), lambda b, i, h: (b, 0, h))],
        out_specs=pl.BlockSpec((tq, hd), lambda b, i, h: (b * tps + i, h)),
        out_shape=jax.ShapeDtypeStruct((rows, d), BF16),
        compiler_params=_params(3),
        name="attn_prompt",
    )(q, k, v)


def _slab_view(x, heads):
    lead = x.shape[:-2]
    hd = x.shape[-1]
    n = len(lead)
    x = x.reshape(lead + (heads, hd // LANES, LANES))
    x = x.transpose(tuple(range(n)) + (n + 1, n, n + 2))
    return x.reshape(lead + (heads * hd // (SUBLANES * LANES), SUBLANES, LANES))


def _slab_unview(x, heads):
    lead = x.shape[:-3]
    n = len(lead)
    tiles = x.shape[-3] * SUBLANES // heads
    x = x.reshape(lead + (tiles, heads, LANES))
    x = x.transpose(tuple(range(n)) + (n + 1, n, n + 2))
    return x.reshape(lead + (heads, tiles * LANES))


def _attn_sample_kernel(q_ref, k_ref, v_ref, o_ref, *, heads, scale, bt):
    for b in range(bt):
        t = jnp.sum(k_ref[0, b] * q_ref[b][None], axis=1)
        t = t + pltpu.roll(t, heads, 1)
        s = jnp.sum(t, axis=-1, keepdims=True) * scale
        mx = jnp.max(s, axis=0, keepdims=True)
        e = jnp.exp(s - mx)
        p = e / jnp.sum(e, axis=0, keepdims=True)
        o_ref[b] = jnp.sum(p[:, None] * v_ref[0, b], axis=0)


def _attn_sample(q, cache_k, cache_v, layer, heads, bt=2):
    bsz, pairs = q.shape[:2]
    m = cache_k.shape[2]
    hd = pairs * SUBLANES * LANES // heads
    kern = functools.partial(_attn_sample_kernel, heads=heads, scale=hd ** -0.5, bt=bt)
    kv_spec = pl.BlockSpec((1, bt, m, pairs, SUBLANES, LANES), lambda i: (layer, i, 0, 0, 0, 0))
    q_spec = pl.BlockSpec((bt, pairs, SUBLANES, LANES), lambda i: (i, 0, 0, 0))
    return pl.pallas_call(
        kern,
        grid=(bsz // bt,),
        in_specs=[q_spec, kv_spec, kv_spec],
        out_specs=q_spec,
        out_shape=jax.ShapeDtypeStruct(q.shape, F32),
        compiler_params=_params(1),
        name="attn_sample",
    )(q, cache_k, cache_v)


def _lru_gates(xc, wa, ba, wi, bi, lam):
    xcb = xc.astype(BF16)
    r = jax.nn.sigmoid(jnp.dot(xcb, wa.astype(BF16), preferred_element_type=F32) + ba)
    ig = jax.nn.sigmoid(jnp.dot(xcb, wi.astype(BF16), preferred_element_type=F32) + bi)
    log_a = -LRU_C * r * jax.nn.softplus(-lam)
    a = jnp.exp(log_a)
    u = jnp.sqrt(-jnp.tanh(log_a) * (a * a + 1.0)) * (ig * xc)
    return a, u


def _lru_prompt_kernel(xb_ref, gt_ref, cw_ref, cb_ref, wa_ref, ba_ref, wi_ref, bi_ref, lam_ref,
                       y_ref, cst_ref, hl_ref, ext_ref, a_ref, u_ref, hs_ref, hc_ref,
                       *, tiles_per_seq, taps):
    tm, cb = xb_ref.shape
    halo = SUBLANES

    @pl.when(pl.program_id(1) % tiles_per_seq == 0)
    def _():
        ext_ref[0:halo, :] = jnp.zeros((halo, cb), F32)
        hc_ref[...] = jnp.zeros((1, cb), F32)

    ext_ref[halo:halo + tm, :] = xb_ref[...]
    cw = cw_ref[...]
    xc = cb_ref[...]
    for k in range(taps):
        start = halo - (taps - 1) + k
        xc = xc + cw[k:k + 1, :] * ext_ref[start:start + tm, :]
    cst_ref[0] = ext_ref[halo + tm - (taps - 1):halo + tm, :]
    ext_ref[0:halo, :] = ext_ref[tm:tm + halo, :]

    a, u = _lru_gates(xc, wa_ref[0], ba_ref[...], wi_ref[0], bi_ref[...], lam_ref[...])

    groups = tm // SUBLANES
    a = a.reshape(groups, SUBLANES, cb)
    u = u.reshape(groups, SUBLANES, cb)
    row = lax.broadcasted_iota(jnp.int32, (groups, SUBLANES, cb), 1)
    s = 1
    while s < SUBLANES:
        keep = row >= s
        u = u + a * jnp.where(keep, pltpu.roll(u, s, 1), 0.0)
        a = a * jnp.where(keep, pltpu.roll(a, s, 1), 1.0)
        s *= 2
    a_ref[...] = a.reshape(tm, cb)
    u_ref[...] = u.reshape(tm, cb)

    def link(g, h):
        off = pl.multiple_of(g * SUBLANES, SUBLANES)
        ag = a_ref[pl.ds(off, SUBLANES), :]
        ug = u_ref[pl.ds(off, SUBLANES), :]
        hs_ref[pl.ds(off, SUBLANES), :] = ag * h + ug
        return ag[SUBLANES - 1:SUBLANES, :] * h + ug[SUBLANES - 1:SUBLANES, :]

    h_last = lax.fori_loop(0, tm // SUBLANES, link, hc_ref[...], unroll=4)
    hc_ref[...] = h_last
    hl_ref[0] = h_last
    y_ref[...] = (hs_ref[...] * jax.nn.gelu(gt_ref[...])).astype(y_ref.dtype)


def _lru_prompt(proj, conv_w, conv_b, w_a, b_a, w_i, b_i, lam, n_seq, seq, tm=1024):
    rows, r2 = proj.shape
    r = r2 // 2
    nb, cb = w_a.shape[0], w_a.shape[1]
    taps = conv_w.shape[0]
    tm = min(tm, seq)
    tps = seq // tm
    kern = functools.partial(_lru_prompt_kernel, tiles_per_seq=tps, taps=taps)
    vec = lambda c, i: (0, c)
    y, cst, hl = pl.pallas_call(
        kern,
        grid=(nb, rows // tm),
        in_specs=[pl.BlockSpec((tm, cb), lambda c, i: (i, c)),
                  pl.BlockSpec((tm, cb), lambda c, i: (i, nb + c)),
                  pl.BlockSpec((taps, cb), vec),
                  pl.BlockSpec((1, cb), vec),
                  pl.BlockSpec((1, cb, cb), lambda c, i: (c, 0, 0)),
                  pl.BlockSpec((1, cb), vec),
                  pl.BlockSpec((1, cb, cb), lambda c, i: (c, 0, 0)),
                  pl.BlockSpec((1, cb), vec),
                  pl.BlockSpec((1, cb), vec)],
        out_specs=[pl.BlockSpec((tm, cb), lambda c, i: (i, c)),
                   pl.BlockSpec((1, taps - 1, cb), lambda c, i: (i // tps, 0, c)),
                   pl.BlockSpec((1, 1, cb), lambda c, i: (i // tps, 0, c))],
        out_shape=[jax.ShapeDtypeStruct((rows, r), BF16),
                   jax.ShapeDtypeStruct((n_seq, taps - 1, r), F32),
                   jax.ShapeDtypeStruct((n_seq, 1, r), F32)],
        scratch_shapes=[pltpu.VMEM((tm + SUBLANES, cb), F32),
                        pltpu.VMEM((tm, cb), F32),
                        pltpu.VMEM((tm, cb), F32),
                        pltpu.VMEM((tm, cb), F32),
                        pltpu.VMEM((1, cb), F32)],
        compiler_params=_params(2),
        name="lru_prompt",
    )(proj, proj, conv_w, conv_b.reshape(1, r), w_a, b_a.reshape(1, r), w_i, b_i.reshape(1, r),
      lam.reshape(1, r))
    return y, cst, hl.reshape(n_seq, r)


def _lru_sample_kernel(xb_ref, gt_ref, c0_ref, c1_ref, c2_ref, h0_ref, cw_ref, cb_ref,
                       wa_ref, ba_ref, wi_ref, bi_ref, lam_ref, y_ref, h_ref):
    cw = cw_ref[...]
    xc = cb_ref[...]
    for k, c_ref in enumerate((c0_ref, c1_ref, c2_ref, xb_ref)):
        xc = xc + cw[k:k + 1, :] * c_ref[...]
    a, u = _lru_gates(xc, wa_ref[0], ba_ref[...], wi_ref[0], bi_ref[...], lam_ref[...])
    h = a * h0_ref[...] + u
    h_ref[...] = h
    y_ref[...] = (h * jax.nn.gelu(gt_ref[...])).astype(y_ref.dtype)


def _lru_sample(proj, conv_rows, h0, conv_w, conv_b, w_a, b_a, w_i, b_i, lam):
    bsz, r2 = proj.shape
    r = r2 // 2
    nb, cb = w_a.shape[0], w_a.shape[1]
    taps = conv_w.shape[0]
    blk = lambda c: (0, c)
    return pl.pallas_call(
        _lru_sample_kernel,
        grid=(nb,),
        in_specs=[pl.BlockSpec((bsz, cb), blk),
                  pl.BlockSpec((bsz, cb), lambda c: (0, nb + c)),
                  pl.BlockSpec((bsz, cb), blk), pl.BlockSpec((bsz, cb), blk),
                  pl.BlockSpec((bsz, cb), blk), pl.BlockSpec((bsz, cb), blk),
                  pl.BlockSpec((taps, cb), blk), pl.BlockSpec((1, cb), blk),
                  pl.BlockSpec((1, cb, cb), lambda c: (c, 0, 0)), pl.BlockSpec((1, cb), blk),
                  pl.BlockSpec((1, cb, cb), lambda c: (c, 0, 0)), pl.BlockSpec((1, cb), blk),
                  pl.BlockSpec((1, cb), blk)],
        out_specs=[pl.BlockSpec((bsz, cb), blk), pl.BlockSpec((bsz, cb), blk)],
        out_shape=[jax.ShapeDtypeStruct((bsz, r), BF16), jax.ShapeDtypeStruct((bsz, r), F32)],
        compiler_params=_params(1),
        name="lru_sample",
    )(proj, proj, conv_rows[0], conv_rows[1], conv_rows[2], h0, conv_w, conv_b.reshape(1, r),
      w_a, b_a.reshape(1, r), w_i, b_i.reshape(1, r), lam.reshape(1, r))


def _pool_prompt_kernel(xn_ref, x_ref, w_ref, b_ref, sc_ref, o_ref, st_ref, ext_ref, wbf_ref,
                        *, tiles_per_seq, windows):
    tm, d = xn_ref.shape
    gd = d // len(windows)
    halo = 2 * SUBLANES
    i = pl.program_id(0)

    @pl.when(i == 0)
    def _():
        wbf_ref[...] = w_ref[...].astype(BF16)

    @pl.when(i % tiles_per_seq == 0)
    def _():
        ext_ref[0:halo, :] = jnp.zeros((halo, d), F32)

    ext_ref[halo:halo + tm, :] = xn_ref[...]
    pos = (i % tiles_per_seq) * tm + lax.broadcasted_iota(jnp.int32, (tm, 1), 0)
    for g, w in enumerate(windows):
        sl = slice(g * gd, (g + 1) * gd)
        s = ext_ref[:, sl]
        shift = 1
        while shift < w:
            s = s + pltpu.roll(s, shift, 0)
            shift *= 2
        cnt = jnp.minimum(pos + 1, w).astype(F32)
        pooled = s[halo:, :] / cnt
        diff = (pooled - xn_ref[:, sl]).astype(BF16)
        mixed = jnp.dot(diff, wbf_ref[g], preferred_element_type=F32) + b_ref[:, sl]
        o_ref[:, sl] = x_ref[:, sl] + mixed * sc_ref[:, sl]
    st_ref[0] = ext_ref[halo + tm - POOL_BUF:halo + tm, :]
    ext_ref[0:halo, :] = ext_ref[tm:tm + halo, :]


def _pool_prompt(xn, x, w, b, scale, n_seq, seq, tm=512):
    rows, d = xn.shape
    tm = min(tm, seq)
    tps = seq // tm
    ng, gd = w.shape[0], w.shape[1]
    kern = functools.partial(_pool_prompt_kernel, tiles_per_seq=tps, windows=POOL_WINDOWS)
    return pl.pallas_call(
        kern,
        grid=(rows // tm,),
        in_specs=[pl.BlockSpec((tm, d), lambda i: (i, 0)),
                  pl.BlockSpec((tm, d), lambda i: (i, 0)),
                  pl.BlockSpec((ng, gd, gd), lambda i: (0, 0, 0)),
                  pl.BlockSpec((1, d), lambda i: (0, 0)),
                  pl.BlockSpec((1, d), lambda i: (0, 0))],
        out_specs=[pl.BlockSpec((tm, d), lambda i: (i, 0)),
                   pl.BlockSpec((1, POOL_BUF, d), lambda i: (i // tps, 0, 0))],
        out_shape=[jax.ShapeDtypeStruct((rows, d), F32),
                   jax.ShapeDtypeStruct((n_seq, POOL_BUF, d), F32)],
        scratch_shapes=[pltpu.VMEM((tm + 2 * SUBLANES, d), F32),
                        pltpu.VMEM((ng, gd, gd), BF16)],
        compiler_params=_params(1),
        name="pool_prompt",
    )(xn, x, w, b.reshape(1, d), scale.reshape(1, d))


def _pool_sample_kernel(xn_ref, x_ref, st_ref, w_ref, b_ref, sc_ref, o_ref, *, window, pos0):
    xn = xn_ref[...]
    s = xn
    for r in range(POOL_BUF - (window - 1), POOL_BUF):
        s = s + st_ref[r]
    pooled = s / float(min(pos0 + 1, window))
    diff = (pooled - xn).astype(BF16)
    mixed = jnp.dot(diff, w_ref[0].astype(BF16), preferred_element_type=F32) + b_ref[...]
    o_ref[...] = x_ref[...] + mixed * sc_ref[...]


def _pool_sample(xn, x, state_t, w, b, scale, pos0):
    bsz, d = xn.shape
    ng, gd = w.shape[0], w.shape[1]
    outs = []
    for g, window in enumerate(POOL_WINDOWS):
        kern = functools.partial(_pool_sample_kernel, window=window, pos0=pos0)
        blk = lambda i, g=g: (0, g)
        outs.append(pl.pallas_call(
            kern,
            grid=(1,),
            in_specs=[pl.BlockSpec((bsz, gd), blk), pl.BlockSpec((bsz, gd), blk),
                      pl.BlockSpec((POOL_BUF, bsz, gd), lambda i, g=g: (0, 0, g)),
                      pl.BlockSpec((1, gd, gd), lambda i, g=g: (g, 0, 0)),
                      pl.BlockSpec((1, gd), blk), pl.BlockSpec((1, gd), blk)],
            out_specs=pl.BlockSpec((bsz, gd), lambda i: (0, 0)),
            out_shape=jax.ShapeDtypeStruct((bsz, gd), F32),
            compiler_params=_params(1),
            name="pool_sample",
        )(xn, x, state_t, w, b.reshape(1, d), scale.reshape(1, d)))
    return jnp.concatenate(outs, axis=-1)


def _s5_matrices(a_re, a_im, b_re, b_im, c_re, c_im, log_dt, chunk):
    g_all, p = a_re.shape
    c = b_re.shape[-1]
    gpb = S5_GROUPS_PER_BLOCK
    nb = g_all // gpb
    dt = jnp.exp(log_dt.astype(F32))[:, None]
    mag = jnp.exp(a_re * dt)
    lr, li = mag * jnp.cos(a_im * dt), mag * jnp.sin(a_im * dt)
    den = a_re * a_re + a_im * a_im
    q_re = ((lr - 1.0) * a_re + li * a_im) / den
    q_im = (li * a_re - (lr - 1.0) * a_im) / den
    bb_re = q_re[..., None] * b_re - q_im[..., None] * b_im
    bb_im = q_re[..., None] * b_im + q_im[..., None] * b_re

    def cmul(xr, xi, yr, yi):
        return xr * yr - xi * yi, xr * yi + xi * yr

    pw = [(jnp.ones_like(lr), jnp.zeros_like(lr))]
    for _ in range(chunk):
        pw.append(cmul(pw[-1][0], pw[-1][1], lr, li))
    hi = lax.Precision.HIGHEST
    lane_g = jnp.arange(LANES) // c
    state_g = jnp.arange(gpb * p) // p
    mask_cc = (lane_g[:, None] == lane_g[None, :]).astype(F32)
    mask_cp = (lane_g[:, None] == state_g[None, :]).astype(F32)
    spread = jnp.tile(jnp.eye(p, dtype=F32), (1, gpb))

    def lanes_by_state(x):
        return x.transpose(0, 2, 1).reshape(nb, LANES, p)

    def state_by_lanes(x):
        return x.transpose(2, 0, 1).reshape(p, nb, LANES).transpose(1, 0, 2)

    def c_lam(tau):
        return cmul(c_re, c_im, pw[tau][0][:, None, :], pw[tau][1][:, None, :])

    bb_rows = jnp.concatenate([lanes_by_state(bb_re), -lanes_by_state(bb_im)], axis=-1)
    blocks = []
    for tau in range(chunk):
        wr, wi = c_lam(tau)
        w_cols = jnp.concatenate([state_by_lanes(wr), state_by_lanes(wi)], axis=1)
        blocks.append(mask_cc * jnp.einsum('brk,bkc->brc', bb_rows, w_cols, precision=hi))
    zero = jnp.zeros_like(blocks[0])
    m_in = jnp.stack([jnp.concatenate([blocks[t - s] if t >= s else zero for t in range(chunk)],
                                      axis=-1) for s in range(chunk)], axis=1)
    m_in = m_in.reshape(nb, chunk * LANES, chunk * LANES).astype(BF16)

    rows_t = []
    for t in range(chunk):
        pr, pi = pw[chunk - 1 - t]
        er, ei = cmul(pr[..., None], pi[..., None], bb_re, bb_im)
        halves = [mask_cp * jnp.einsum('brp,pq->brq', lanes_by_state(e), spread, precision=hi)
                  for e in (er, ei)]
        rows_t.append(jnp.concatenate(halves, axis=-1))
    m_state = jnp.stack(rows_t, axis=1).reshape(nb, chunk * LANES, 2 * gpb * p).astype(BF16)

    cols_t = []
    for t in range(chunk):
        wr, wi = c_lam(t + 1)
        halves = [mask_cp.T * jnp.einsum('qp,bpc->bqc', spread.T, state_by_lanes(w), precision=hi)
                  for w in (wr, -wi)]
        cols_t.append(jnp.concatenate(halves, axis=1))
    m_out = jnp.concatenate(cols_t, axis=-1).astype(BF16)

    qr, qi = pw[chunk]
    rows = [(jnp.ones_like(qr), jnp.zeros_like(qr))]
    for _ in range(SUBLANES):
        rows.append(cmul(rows[-1][0], rows[-1][1], qr, qi))
    table = jnp.stack([_s5_state_layout(r_, i_) for r_, i_ in rows])
    table = jnp.concatenate([table, jnp.zeros((2 * SUBLANES - table.shape[0], table.shape[1]), F32)])
    return m_in, m_state, m_out, table


def _s5_state_layout(re, im):
    lead = re.shape[:-2]
    g_all, p = re.shape[-2:]
    nb = g_all // S5_GROUPS_PER_BLOCK
    st = jnp.stack([re.reshape(lead + (nb, S5_GROUPS_PER_BLOCK * p)),
                    im.reshape(lead + (nb, S5_GROUPS_PER_BLOCK * p))], axis=-2)
    return st.reshape(lead + (2 * g_all * p,))


def _s5_state_unlayout(flat, g_all, p):
    lead = flat.shape[:-1]
    nb = g_all // S5_GROUPS_PER_BLOCK
    st = flat.reshape(lead + (nb, 2, S5_GROUPS_PER_BLOCK * p))
    return st[..., 0, :].reshape(lead + (g_all, p)), st[..., 1, :].reshape(lead + (g_all, p))


def _s5_lhs(u_refs):
    return jnp.concatenate([u[...] for u in u_refs], axis=1).astype(BF16)


def _s5_state_in_kernel(*refs, chunk):
    u_refs, m_ref, e_ref = refs[:chunk], refs[chunk], refs[chunk + 1]
    e_ref[...] = jnp.dot(_s5_lhs(u_refs), m_ref[0], preferred_element_type=F32)


def _s5_state_in(xnv, m_state, chunk, d, tr=1024):
    r = xnv.shape[0]
    nb, _, sw = m_state.shape
    tr = min(tr, r)
    lb = d // LANES
    kern = functools.partial(_s5_state_in_kernel, chunk=chunk)
    return pl.pallas_call(
        kern,
        grid=(nb, r // tr),
        in_specs=[pl.BlockSpec((tr, LANES), lambda b, i, t=t: (i, t * lb + b)) for t in range(chunk)]
        + [pl.BlockSpec((1, chunk * LANES, sw), lambda b, i: (b, 0, 0))],
        out_specs=pl.BlockSpec((tr, sw), lambda b, i: (i, b)),
        out_shape=jax.ShapeDtypeStruct((r, nb * sw), F32),
        compiler_params=_params(2),
        name="s5_state_in",
    )(*([xnv] * chunk), m_state)


def _s5_scan_kernel(e_ref, tab_ref, h_ref, fin_ref):
    rps, sw = e_ref.shape
    half = sw // 2
    groups = rps // SUBLANES
    er = e_ref[:, :half].reshape(groups, SUBLANES, half)
    ei = e_ref[:, half:].reshape(groups, SUBLANES, half)
    row = lax.broadcasted_iota(jnp.int32, (groups, SUBLANES, half), 1)
    s = 1
    while s < SUBLANES:
        keep = row >= s
        pr, pi = tab_ref[s:s + 1, :half], tab_ref[s:s + 1, half:]
        sr = jnp.where(keep, pltpu.roll(er, s, 1), 0.0)
        si = jnp.where(keep, pltpu.roll(ei, s, 1), 0.0)
        er, ei = er + pr * sr - pi * si, ei + pr * si + pi * sr
        s *= 2
    xr = jnp.where(row >= 1, pltpu.roll(er, 1, 1), 0.0)
    xi = jnp.where(row >= 1, pltpu.roll(ei, 1, 1), 0.0)
    ar, ai = tab_ref[0:SUBLANES, :half], tab_ref[0:SUBLANES, half:]
    l8r, l8i = tab_ref[SUBLANES:SUBLANES + 1, :half], tab_ref[SUBLANES:SUBLANES + 1, half:]
    cr = jnp.zeros((1, half), F32)
    ci = jnp.zeros((1, half), F32)
    for g in range(groups):
        lo, hi = g * SUBLANES, (g + 1) * SUBLANES
        h_ref[lo:hi, :half] = ar * cr - ai * ci + xr[g]
        h_ref[lo:hi, half:] = ar * ci + ai * cr + xi[g]
        cr, ci = (l8r * cr - l8i * ci + er[g, SUBLANES - 1:SUBLANES],
                  l8r * ci + l8i * cr + ei[g, SUBLANES - 1:SUBLANES])
    fin_ref[0, :, :half] = cr
    fin_ref[0, :, half:] = ci


def _s5_scan(e, table, n_seq, sw):
    r, width = e.shape
    rps = r // n_seq
    return pl.pallas_call(
        _s5_scan_kernel,
        grid=(width // sw, n_seq),
        in_specs=[pl.BlockSpec((rps, sw), lambda c, b: (b, c)),
                  pl.BlockSpec((2 * SUBLANES, sw), lambda c, b: (0, c))],
        out_specs=[pl.BlockSpec((rps, sw), lambda c, b: (b, c)),
                   pl.BlockSpec((1, 1, sw), lambda c, b: (b, 0, c))],
        out_shape=[jax.ShapeDtypeStruct((r, width), F32),
                   jax.ShapeDtypeStruct((n_seq, 1, width), F32)],
        compiler_params=_params(2),
        name="s5_scan",
    )(e, table)


def _s5_step_kernel(e_ref, h0_ref, tab_ref, o_ref):
    half = e_ref.shape[1] // 2
    lr, li = tab_ref[1:2, :half], tab_ref[1:2, half:]
    hr, hi = h0_ref[:, :half], h0_ref[:, half:]
    o_ref[:, :half] = lr * hr - li * hi + e_ref[:, :half]
    o_ref[:, half:] = lr * hi + li * hr + e_ref[:, half:]


def _s5_step(e, h0, table, sw):
    bsz, width = e.shape
    return pl.pallas_call(
        _s5_step_kernel,
        grid=(width // sw,),
        in_specs=[pl.BlockSpec((bsz, sw), lambda c: (0, c)),
                  pl.BlockSpec((bsz, sw), lambda c: (0, c)),
                  pl.BlockSpec((2 * SUBLANES, sw), lambda c: (0, c))],
        out_specs=pl.BlockSpec((bsz, sw), lambda c: (0, c)),
        out_shape=jax.ShapeDtypeStruct((bsz, width), F32),
        compiler_params=_params(1),
        name="s5_step",
    )(e, h0, table)


def _s5_out_kernel(*refs, chunk):
    u_refs = refs[:chunk]
    h_ref, mi_ref, mo_ref = refs[chunk:chunk + 3]
    y_refs = refs[chunk + 3:]
    y = jnp.dot(_s5_lhs(u_refs), mi_ref[0], preferred_element_type=F32)
    y = y + jnp.dot(h_ref[...].astype(BF16), mo_ref[0], preferred_element_type=F32)
    for t, y_ref in enumerate(y_refs):
        y_ref[...] = y[:, t * LANES:(t + 1) * LANES]


def _s5_out(xnv, h, m_in, m_out, chunk, d, tr=1024):
    r = xnv.shape[0]
    nb, sw, _ = m_out.shape
    tr = min(tr, r)
    lb = d // LANES
    kern = functools.partial(_s5_out_kernel, chunk=chunk)
    return pl.pallas_call(
        kern,
        grid=(nb, r // tr),
        in_specs=[pl.BlockSpec((tr, LANES), lambda b, i, t=t: (i, t * lb + b)) for t in range(chunk)]
        + [pl.BlockSpec((tr, sw), lambda b, i: (i, b)),
           pl.BlockSpec((1, chunk * LANES, chunk * LANES), lambda b, i: (b, 0, 0)),
           pl.BlockSpec((1, sw, chunk * LANES), lambda b, i: (b, 0, 0))],
        out_specs=[pl.BlockSpec((tr, LANES), lambda b, i: (i, b)) for _ in range(chunk)],
        out_shape=[jax.ShapeDtypeStruct((r, d), F32) for _ in range(chunk)],
        compiler_params=_params(2),
        name="s5_out",
    )(*([xnv] * chunk), h, m_in, m_out)


def _s5_glu_lhs(y, xn, dvec):
    return jax.nn.gelu(y + dvec * xn)


def _s5_mixer(x, xn, h0_flat, n_seq, prm, w_glu, b_glu, layer, chunk):
    a_re, a_im, b_re, b_im, c_re, c_im, dvec, log_dt = prm
    rows, d = xn.shape
    m_in, m_state, m_out, table = _s5_matrices(a_re, a_im, b_re, b_im, c_re, c_im, log_dt, chunk)
    sw = m_state.shape[-1]
    xnv = xn.reshape(rows // chunk, chunk * d)
    e = _s5_state_in(xnv, m_state, chunk, d)
    if h0_flat is None:
        h, fin = _s5_scan(e, table, n_seq, sw)
        fin = fin.reshape(n_seq, -1)
    else:
        h = h0_flat
        fin = _s5_step(e, h0_flat, table, sw)
    ys = _s5_out(xnv, h, m_in, m_out, chunk, d)
    y = jnp.stack(ys, axis=1).reshape(rows, d)
    b_glu3 = b_glu.reshape(b_glu.shape[0], 1, 2 * d)
    tiles = dict(tm=512, tn=1024, single_w=True) if h0_flat is None else {}
    x_new = _mm(y, [(w_glu, layer, 0), (w_glu, layer, d)], d, _ep_glu_residual, F32,
                lhs_fn=_s5_glu_lhs, lhs_extra=[xn, dvec.reshape(1, d)],
                rows_extra=[(b_glu3, layer, 0), (b_glu3, layer, d)],
                full_extra=[x], name="s5_glu", **tiles)
    return x_new, fin


def kernel(x_prompt, x_sample, cache_mem_k, cache_mem_v, state_lru_conv, state_lru_h, state_pool, state_s5_re, state_s5_im, mem_prompt, g_mix, g_xattn, g_mem, g_mlp, g_final, w_q, w_k, w_v, w_o, w_up, w_down, lru_w_in, lru_conv_w, lru_conv_b, lru_w_a, lru_b_a, lru_w_i, lru_b_i, lru_lambda, lru_w_o, pool_w, pool_b, pool_scale, s5_a_re, s5_a_im, s5_b_re, s5_b_im, s5_c_re, s5_c_im, s5_d, s5_log_dt, s5_w_glu, s5_b_glu):
    n_seq, seq, d = x_prompt.shape
    bsz, dec_seq, _ = x_sample.shape
    assert dec_seq == 1, "the sample group advances one token per request"
    depth = g_mix.shape[0]
    mem_tokens = mem_prompt.shape[1]
    heads, head_dim = cache_mem_k.shape[3], cache_mem_k.shape[4]
    g_all, p_state = s5_a_re.shape[1], s5_a_re.shape[2]
    past_len = 16384

    xp = x_prompt.reshape(n_seq * seq, d)
    xs = x_sample.reshape(bsz, d)
    mem = mem_prompt.reshape(n_seq * mem_tokens, d)
    wide = dict(tm=512, tn=d, single_w=True)
    assert SUBLANES == 2 * heads and head_dim % (2 * LANES) == 0, "slab view needs 4 heads"
    cache_k_view = _slab_view(cache_mem_k, heads)
    cache_v_view = _slab_view(cache_mem_v, heads)

    mem_k, mem_v = [], []
    lru_conv_p, lru_h_p, lru_conv_s, lru_h_s = [], [], [], []
    pool_p, pool_s = [], []
    s5_re_p, s5_im_p, s5_re_s, s5_im_s = [], [], [], []

    for i in range(depth):
        kind, j = i % 3, i // 3
        if kind == 0:
            prm = (lru_conv_w[j], lru_conv_b[j], lru_w_a[j], lru_b_a[j], lru_w_i[j], lru_b_i[j],
                   lru_lambda[j])
            proj = _mm(xp, [(lru_w_in, j, 0)], 2 * d, _ep_plain, F32, norm_g=g_mix[i],
                       name="lru_in", **wide)
            y, cst, hl = _lru_prompt(proj, *prm, n_seq, seq)
            xp = _mm(y, [(lru_w_o, j, 0)], d, _ep_residual, F32, full_extra=[xp], name="lru_out",
                     **wide)
            lru_conv_p.append(cst)
            lru_h_p.append(hl)

            proj = _mm(xs, [(lru_w_in, j, 0)], 2 * d, _ep_plain, F32, norm_g=g_mix[i],
                       name="lru_in_s")
            conv_rows = [state_lru_conv[j][:, r, :] for r in range(state_lru_conv.shape[2])]
            y, h_new = _lru_sample(proj, conv_rows, state_lru_h[j], *prm)
            xs = _mm(y, [(lru_w_o, j, 0)], d, _ep_residual, F32, full_extra=[xs], name="lru_out_s")
            lru_conv_s.append(jnp.concatenate(
                [state_lru_conv[j][:, 1:, :], proj[:, None, :d]], axis=1))
            lru_h_s.append(h_new)
        elif kind == 1:
            xn = _rmsnorm(xp, g_mix[i], F32)
            xp, st = _pool_prompt(xn, xp, pool_w[j], pool_b[j], pool_scale[j], n_seq, seq)
            pool_p.append(st)

            xn = _rmsnorm(xs, g_mix[i], F32)
            xs = _pool_sample(xn, xs, jnp.swapaxes(state_pool[j], 0, 1), pool_w[j], pool_b[j],
                              pool_scale[j], past_len)
            pool_s.append(jnp.concatenate([state_pool[j][:, 1:, :], xn[:, None, :]], axis=1))
        else:
            prm = (s5_a_re[j], s5_a_im[j], s5_b_re[j], s5_b_im[j], s5_c_re[j], s5_c_im[j],
                   s5_d[j], s5_log_dt[j])
            xp, fin = _s5_mixer(xp, _rmsnorm(xp, g_mix[i], F32), None, n_seq, prm,
                                s5_w_glu, s5_b_glu, j, S5_CHUNK)
            re, im = _s5_state_unlayout(fin, g_all, p_state)
            s5_re_p.append(re)
            s5_im_p.append(im)

            h0 = _s5_state_layout(state_s5_re[j], state_s5_im[j])
            xs, fin = _s5_mixer(xs, _rmsnorm(xs, g_mix[i], F32), h0, bsz, prm,
                                s5_w_glu, s5_b_glu, j, 1)
            re, im = _s5_state_unlayout(fin, g_all, p_state)
            s5_re_s.append(re)
            s5_im_s.append(im)

        k = _mm(mem, [(w_k, i, 0)], d, _ep_plain, F32, norm_g=g_mem[i], name="mem_k")
        v = _mm(mem, [(w_v, i, 0)], d, _ep_plain, F32, norm_g=g_mem[i], name="mem_v")
        mem_k.append(k.reshape(n_seq, mem_tokens, heads, head_dim))
        mem_v.append(v.reshape(n_seq, mem_tokens, heads, head_dim))
        q = _mm(xp, [(w_q, i, 0)], d, _ep_plain, BF16, norm_g=g_xattn[i], name="q", **wide)
        o = _attn_prompt(q, k.reshape(n_seq, mem_tokens, d), v.reshape(n_seq, mem_tokens, d),
                         n_seq, seq, heads)
        xp = _mm(o, [(w_o, i, 0)], d, _ep_residual, F32, full_extra=[xp], name="attn_out", **wide)

        q = _mm(xs, [(w_q, i, 0)], d, _ep_plain, F32, norm_g=g_xattn[i], name="q_s")
        o = _attn_sample(_slab_view(q.reshape(bsz, heads, head_dim), heads), cache_k_view,
                         cache_v_view, i, heads)
        xs = _mm(_slab_unview(o, heads).reshape(bsz, d), [(w_o, i, 0)], d, _ep_residual, F32,
                 full_extra=[xs], name="attn_out_s")

        xs, wu_bf, wd_bf = _mlp_cast(xs, g_mlp[i], w_up, w_down, i)
        xp = _mlp(xp, g_mlp[i], wu_bf, wd_bf)

    y_prompt = _rmsnorm(xp, g_final, F32).reshape(n_seq, seq, d)
    y_sample = _rmsnorm(xs, g_final, F32).reshape(bsz, 1, d)
    return (y_prompt, y_sample,
            jnp.stack(mem_k), jnp.stack(mem_v),
            jnp.stack(lru_conv_p), jnp.stack(lru_h_p), jnp.stack(pool_p),
            jnp.stack(s5_re_p), jnp.stack(s5_im_p),
            jnp.stack(lru_conv_s), jnp.stack(lru_h_s), jnp.stack(pool_s),
            jnp.stack(s5_re_s), jnp.stack(s5_im_s))
```

```python
import functools
import math

import jax
import jax.numpy as jnp
from jax import lax
from jax.experimental import pallas as pl
from jax.experimental.pallas import tpu as pltpu

F32 = jnp.float32
BF16 = jnp.bfloat16

SUBLANES = 8
LANES = 128
VMEM_LIMIT_BYTES = 56 * 1024 * 1024

RMS_EPS = 1e-6
LRU_C = 8.0
POOL_WINDOWS = (2, 4, 8, 16)
POOL_BUF = max(POOL_WINDOWS) - 1
S5_GROUP_DIM = 16
S5_CHUNK = 8
S5_GROUPS_PER_BLOCK = LANES // S5_GROUP_DIM


def _params(n_axes):
    return pltpu.CompilerParams(dimension_semantics=("arbitrary",) * n_axes,
                                vmem_limit_bytes=VMEM_LIMIT_BYTES)


def _rms(x, g):
    xf = x.astype(F32)
    inv = lax.rsqrt(jnp.mean(xf * xf, axis=-1, keepdims=True) + RMS_EPS)
    return xf * inv * g.astype(F32)


def _rmsnorm_kernel(x_ref, g_ref, o_ref):
    o_ref[...] = _rms(x_ref[...], g_ref[0]).astype(o_ref.dtype)


def _rmsnorm(x, gain, out_dtype, tm=512):
    rows, d = x.shape
    tm = min(tm, rows)
    g, layer = gain
    return pl.pallas_call(
        _rmsnorm_kernel,
        grid=(rows // tm,),
        in_specs=[pl.BlockSpec((tm, d), lambda i: (i, 0)),
                  pl.BlockSpec((1, 1, d), lambda i: (layer, 0, 0))],
        out_specs=pl.BlockSpec((tm, d), lambda i: (i, 0)),
        out_shape=jax.ShapeDtypeStruct((rows, d), out_dtype),
        compiler_params=_params(1),
        name="rmsnorm",
    )(x, g)


def _mm_kernel(*refs, n_w, n_row, n_full, has_norm, n_lhs, lhs_fn, epilogue):
    x_ref = refs[0]
    refs = refs[1:]
    g_ref = refs[0] if has_norm else None
    refs = refs[1:] if has_norm else refs
    lhs_refs = refs[:n_lhs]
    refs = refs[n_lhs:]
    w_refs = refs[:n_w]
    row_refs = refs[n_w:n_w + n_row]
    full_refs = refs[n_w + n_row:n_w + n_row + n_full]
    o_ref = refs[n_w + n_row + n_full]
    wbf_refs = refs[1 + n_w + n_row + n_full:]

    @pl.when(pl.program_id(1) == 0)
    def _():
        for w_ref, wbf_ref in zip(w_refs, wbf_refs):
            wbf_ref[...] = w_ref[0].astype(BF16)

    if has_norm:
        xb = _rms(x_ref[...], g_ref[0]).astype(BF16)
    elif lhs_fn is not None:
        xb = lhs_fn(x_ref[...], *[r[...] for r in lhs_refs]).astype(BF16)
    else:
        xb = x_ref[...].astype(BF16)
    accs = [jnp.dot(xb, wbf_ref[...], preferred_element_type=F32) for wbf_ref in wbf_refs]
    out = epilogue(accs, [r[0] for r in row_refs], [f[...] for f in full_refs])
    o_ref[...] = out.astype(o_ref.dtype)


def _mm(x, ws, n_cols, epilogue, out_dtype, *, norm_g=None, lhs_fn=None, lhs_extra=(),
        rows_extra=(), full_extra=(), tm=1024, tn=512, single_w=False, name="mm"):
    rows, k = x.shape
    tm = min(tm, rows)
    tn = min(tn, n_cols)
    in_specs = [pl.BlockSpec((tm, k), lambda j, i: (i, 0))]
    args = [x]
    if norm_g is not None:
        g, g_layer = norm_g
        in_specs.append(pl.BlockSpec((1, 1, k), lambda j, i: (g_layer, 0, 0)))
        args.append(g)
    for e in lhs_extra:
        if e.shape[0] == 1:
            in_specs.append(pl.BlockSpec((1, k), lambda j, i: (0, 0)))
        else:
            in_specs.append(pl.BlockSpec((tm, k), lambda j, i: (i, 0)))
        args.append(e)
    w_mode = dict(pipeline_mode=pl.Buffered(1)) if single_w else {}
    for w, layer, off in ws:
        in_specs.append(pl.BlockSpec((1, k, tn), lambda j, i, l=layer, o=off // tn: (l, 0, o + j),
                                     **w_mode))
        args.append(w)
    for v, layer, off in rows_extra:
        in_specs.append(pl.BlockSpec((1, 1, tn), lambda j, i, l=layer, o=off // tn: (l, 0, o + j)))
        args.append(v)
    for f in full_extra:
        in_specs.append(pl.BlockSpec((tm, tn), lambda j, i: (i, j)))
        args.append(f)
    kern = functools.partial(_mm_kernel, n_w=len(ws), n_row=len(rows_extra),
                             n_full=len(full_extra), has_norm=norm_g is not None,
                             n_lhs=len(lhs_extra), lhs_fn=lhs_fn, epilogue=epilogue)
    return pl.pallas_call(
        kern,
        grid=(n_cols // tn, rows // tm),
        in_specs=in_specs,
        out_specs=pl.BlockSpec((tm, tn), lambda j, i: (i, j)),
        out_shape=jax.ShapeDtypeStruct((rows, n_cols), out_dtype),
        scratch_shapes=[pltpu.VMEM((k, tn), BF16) for _ in ws],
        compiler_params=_params(2),
        name=name,
    )(*args)


def _ep_plain(accs, rows, fulls):
    return accs[0]


def _ep_residual(accs, rows, fulls):
    return fulls[0] + accs[0]


def _ep_glu_residual(accs, rows, fulls):
    a = accs[0] + rows[0]
    b = accs[1] + rows[1]
    return fulls[0] + a * jax.nn.sigmoid(b)


def _mlp_kernel(x_ref, g_ref, wu_ref, wd_ref, o_ref, xn_ref):
    @pl.when(pl.program_id(1) == 0)
    def _():
        xv = x_ref[...]
        xn_ref[...] = _rms(xv, g_ref[0]).astype(BF16)
        o_ref[...] = xv

    h = jnp.dot(xn_ref[...], wu_ref[...], preferred_element_type=F32)
    h = jnp.maximum(h, 0.0)
    o_ref[...] += jnp.dot((h * h).astype(BF16), wd_ref[...], preferred_element_type=F32)


def _mlp(x, gain, wu_bf, wd_bf, tm=1024, tf=512):
    rows, d = x.shape
    dff = wu_bf.shape[1]
    tm = min(tm, rows)
    g, layer = gain
    return pl.pallas_call(
        _mlp_kernel,
        grid=(rows // tm, dff // tf),
        in_specs=[pl.BlockSpec((tm, d), lambda i, f: (i, 0)),
                  pl.BlockSpec((1, 1, d), lambda i, f: (layer, 0, 0)),
                  pl.BlockSpec((d, tf), lambda i, f: (0, f)),
                  pl.BlockSpec((tf, d), lambda i, f: (f, 0))],
        out_specs=pl.BlockSpec((tm, d), lambda i, f: (i, 0)),
        out_shape=jax.ShapeDtypeStruct((rows, d), F32),
        scratch_shapes=[pltpu.VMEM((tm, d), BF16)],
        compiler_params=_params(2),
        name="mlp",
    )(x, g, wu_bf, wd_bf)


def _mlp_cast_kernel(x_ref, g_ref, wu_ref, wd_ref, o_ref, wub_ref, wdb_ref, xn_ref):
    @pl.when(pl.program_id(0) == 0)
    def _():
        xv = x_ref[...]
        xn_ref[...] = _rms(xv, g_ref[0]).astype(BF16)
        o_ref[...] = xv

    wu = wu_ref[0].astype(BF16)
    wd = wd_ref[0].astype(BF16)
    wub_ref[...] = wu
    wdb_ref[...] = wd
    h = jnp.dot(xn_ref[...], wu, preferred_element_type=F32)
    h = jnp.maximum(h, 0.0)
    o_ref[...] += jnp.dot((h * h).astype(BF16), wd, preferred_element_type=F32)


def _mlp_cast(x, gain, w_up, w_down, layer, tf=512):
    rows, d = x.shape
    dff = w_up.shape[2]
    g, g_layer = gain
    return pl.pallas_call(
        _mlp_cast_kernel,
        grid=(dff // tf,),
        in_specs=[pl.BlockSpec((rows, d), lambda f: (0, 0)),
                  pl.BlockSpec((1, 1, d), lambda f: (g_layer, 0, 0)),
                  pl.BlockSpec((1, d, tf), lambda f: (layer, 0, f)),
                  pl.BlockSpec((1, tf, d), lambda f: (layer, f, 0))],
        out_specs=[pl.BlockSpec((rows, d), lambda f: (0, 0)),
                   pl.BlockSpec((d, tf), lambda f: (0, f)),
                   pl.BlockSpec((tf, d), lambda f: (f, 0))],
        out_shape=[jax.ShapeDtypeStruct((rows, d), F32),
                   jax.ShapeDtypeStruct((d, dff), BF16),
                   jax.ShapeDtypeStruct((dff, d), BF16)],
        scratch_shapes=[pltpu.VMEM((rows, d), BF16)],
        compiler_params=_params(1),
        name="mlp_cast",
    )(x, g, w_up, w_down)


def _attn_prompt_kernel(q_ref, k_ref, v_ref, o_ref, *, scale):
    q = q_ref[...]
    k = k_ref[0].astype(BF16)
    s = lax.dot_general(q, k, (((1,), (1,)), ((), ())), preferred_element_type=F32) * scale
    m = jnp.max(s, axis=-1, keepdims=True)
    e = jnp.exp(s - m)
    p = e / jnp.sum(e, axis=-1, keepdims=True)
    o = jnp.dot(p.astype(BF16), v_ref[0].astype(BF16), preferred_element_type=F32)
    o_ref[...] = o.astype(o_ref.dtype)


def _attn_prompt(q, k, v, n_seq, seq, heads, tq=1024):
    rows, d = q.shape
    m = k.shape[1]
    hd = d // heads
    tq = min(tq, seq)
    tps = seq // tq
    kern = functools.partial(_attn_prompt_kernel, scale=hd ** -0.5)
    return pl.pallas_call(
        kern,
        grid=(n_seq, tps, heads),
        in_specs=[pl.BlockSpec((tq, hd), lambda b, i, h: (b * tps + i, h)),
                  pl.BlockSpec((1, m, hd), lambda b, i, h: (b, 0, h)),
                  pl.BlockSpec((1, m, hd), lambda b, i, h: (b, 0, h))],
        out_specs=pl.BlockSpec((tq, hd), lambda b, i, h: (b * tps + i, h)),
        out_shape=jax.ShapeDtypeStruct((rows, d), BF16),
        compiler_params=_params(3),
        name="attn_prompt",
    )(q, k, v)


def _slab_view(x, heads):
    lead = x.shape[:-2]
    hd = x.shape[-1]
    n = len(lead)
    x = x.reshape(lead + (heads, hd // LANES, LANES))
    x = x.transpose(tuple(range(n)) + (n + 1, n, n + 2))
    return x.reshape(lead + (heads * hd // (SUBLANES * LANES), SUBLANES, LANES))


def _slab_unview(x, heads):
    lead = x.shape[:-3]
    n = len(lead)
    tiles = x.shape[-3] * SUBLANES // heads
    x = x.reshape(lead + (tiles, heads, LANES))
    x = x.transpose(tuple(range(n)) + (n + 1, n, n + 2))
    return x.reshape(lead + (heads, tiles * LANES))


def _attn_sample_kernel(q_ref, k_ref, v_ref, o_ref, *, heads, scale, bt):
    for b in range(bt):
        t = jnp.sum(k_ref[0, b] * q_ref[b][None], axis=1)
        t = t + pltpu.roll(t, heads, 1)
        s = jnp.sum(t, axis=-1, keepdims=True) * scale
        mx = jnp.max(s, axis=0, keepdims=True)
        e = jnp.exp(s - mx)
        p = e / jnp.sum(e, axis=0, keepdims=True)
        o_ref[b] = jnp.sum(p[:, None] * v_ref[0, b], axis=0)


def _attn_sample(q, cache_k, cache_v, layer, heads, bt=2):
    bsz, pairs = q.shape[:2]
    m = cache_k.shape[2]
    hd = pairs * SUBLANES * LANES // heads
    kern = functools.partial(_attn_sample_kernel, heads=heads, scale=hd ** -0.5, bt=bt)
    kv_spec = pl.BlockSpec((1, bt, m, pairs, SUBLANES, LANES), lambda i: (layer, i, 0, 0, 0, 0))
    q_spec = pl.BlockSpec((bt, pairs, SUBLANES, LANES), lambda i: (i, 0, 0, 0))
    return pl.pallas_call(
        kern,
        grid=(bsz // bt,),
        in_specs=[q_spec, kv_spec, kv_spec],
        out_specs=q_spec,
        out_shape=jax.ShapeDtypeStruct(q.shape, F32),
        compiler_params=_params(1),
        name="attn_sample",
    )(q, cache_k, cache_v)


def _lru_gates(xc, wa, ba, wi, bi, lam):
    xcb = xc.astype(BF16)
    r = jax.nn.sigmoid(jnp.dot(xcb, wa.astype(BF16), preferred_element_type=F32) + ba)
    ig = jax.nn.sigmoid(jnp.dot(xcb, wi.astype(BF16), preferred_element_type=F32) + bi)
    log_a = -LRU_C * r * jax.nn.softplus(-lam)
    a = jnp.exp(log_a)
    u = jnp.sqrt(-jnp.tanh(log_a) * (a * a + 1.0)) * (ig * xc)
    return a, u


def _lru_prompt_kernel(xb_ref, gt_ref, cw_ref, cb_ref, wa_ref, ba_ref, wi_ref, bi_ref, lam_ref,
                       y_ref, cst_ref, hl_ref, ext_ref, a_ref, u_ref, hs_ref, hc_ref,
                       *, tiles_per_seq, taps):
    tm, cb = xb_ref.shape
    halo = SUBLANES

    @pl.when(pl.program_id(1) % tiles_per_seq == 0)
    def _():
        ext_ref[0:halo, :] = jnp.zeros((halo, cb), F32)
        hc_ref[...] = jnp.zeros((1, cb), F32)

    ext_ref[halo:halo + tm, :] = xb_ref[...]
    cw = cw_ref[...]
    xc = cb_ref[...]
    for k in range(taps):
        start = halo - (taps - 1) + k
        xc = xc + cw[k:k + 1, :] * ext_ref[start:start + tm, :]
    cst_ref[0] = ext_ref[halo + tm - (taps - 1):halo + tm, :]
    ext_ref[0:halo, :] = ext_ref[tm:tm + halo, :]

    a, u = _lru_gates(xc, wa_ref[0], ba_ref[...], wi_ref[0], bi_ref[...], lam_ref[...])

    groups = tm // SUBLANES
    a = a.reshape(groups, SUBLANES, cb)
    u = u.reshape(groups, SUBLANES, cb)
    row = lax.broadcasted_iota(jnp.int32, (groups, SUBLANES, cb), 1)
    s = 1
    while s < SUBLANES:
        keep = row >= s
        u = u + a * jnp.where(keep, pltpu.roll(u, s, 1), 0.0)
        a = a * jnp.where(keep, pltpu.roll(a, s, 1), 1.0)
        s *= 2
    a_ref[...] = a.reshape(tm, cb)
    u_ref[...] = u.reshape(tm, cb)

    def link(g, h):
        off = pl.multiple_of(g * SUBLANES, SUBLANES)
        ag = a_ref[pl.ds(off, SUBLANES), :]
        ug = u_ref[pl.ds(off, SUBLANES), :]
        hs_ref[pl.ds(off, SUBLANES), :] = ag * h + ug
        return ag[SUBLANES - 1:SUBLANES, :] * h + ug[SUBLANES - 1:SUBLANES, :]

    h_last = lax.fori_loop(0, tm // SUBLANES, link, hc_ref[...], unroll=4)
    hc_ref[...] = h_last
    hl_ref[0] = h_last
    y_ref[...] = (hs_ref[...] * jax.nn.gelu(gt_ref[...])).astype(y_ref.dtype)


def _lru_prompt(proj, conv_w, conv_b, w_a, b_a, w_i, b_i, lam, n_seq, seq, tm=1024):
    rows, r2 = proj.shape
    r = r2 // 2
    nb, cb = w_a.shape[0], w_a.shape[1]
    taps = conv_w.shape[0]
    tm = min(tm, seq)
    tps = seq // tm
    kern = functools.partial(_lru_prompt_kernel, tiles_per_seq=tps, taps=taps)
    vec = lambda c, i: (0, c)
    y, cst, hl = pl.pallas_call(
        kern,
        grid=(nb, rows // tm),
        in_specs=[pl.BlockSpec((tm, cb), lambda c, i: (i, c)),
                  pl.BlockSpec((tm, cb), lambda c, i: (i, nb + c)),
                  pl.BlockSpec((taps, cb), vec),
                  pl.BlockSpec((1, cb), vec),
                  pl.BlockSpec((1, cb, cb), lambda c, i: (c, 0, 0)),
                  pl.BlockSpec((1, cb), vec),
                  pl.BlockSpec((1, cb, cb), lambda c, i: (c, 0, 0)),
                  pl.BlockSpec((1, cb), vec),
                  pl.BlockSpec((1, cb), vec)],
        out_specs=[pl.BlockSpec((tm, cb), lambda c, i: (i, c)),
                   pl.BlockSpec((1, taps - 1, cb), lambda c, i: (i // tps, 0, c)),
                   pl.BlockSpec((1, 1, cb), lambda c, i: (i // tps, 0, c))],
        out_shape=[jax.ShapeDtypeStruct((rows, r), BF16),
                   jax.ShapeDtypeStruct((n_seq, taps - 1, r), F32),
                   jax.ShapeDtypeStruct((n_seq, 1, r), F32)],
        scratch_shapes=[pltpu.VMEM((tm + SUBLANES, cb), F32),
                        pltpu.VMEM((tm, cb), F32),
                        pltpu.VMEM((tm, cb), F32),
                        pltpu.VMEM((tm, cb), F32),
                        pltpu.VMEM((1, cb), F32)],
        compiler_params=_params(2),
        name="lru_prompt",
    )(proj, proj, conv_w, conv_b.reshape(1, r), w_a, b_a.reshape(1, r), w_i, b_i.reshape(1, r),
      lam.reshape(1, r))
    return y, cst, hl.reshape(n_seq, r)


def _lru_sample_kernel(xb_ref, gt_ref, c0_ref, c1_ref, c2_ref, h0_ref, cw_ref, cb_ref,
                       wa_ref, ba_ref, wi_ref, bi_ref, lam_ref, y_ref, h_ref):
    cw = cw_ref[...]
    xc = cb_ref[...]
    for k, c_ref in enumerate((c0_ref, c1_ref, c2_ref, xb_ref)):
        xc = xc + cw[k:k + 1, :] * c_ref[...]
    a, u = _lru_gates(xc, wa_ref[0], ba_ref[...], wi_ref[0], bi_ref[...], lam_ref[...])
    h = a * h0_ref[...] + u
    h_ref[...] = h
    y_ref[...] = (h * jax.nn.gelu(gt_ref[...])).astype(y_ref.dtype)


def _lru_sample(proj, conv_rows, h0, conv_w, conv_b, w_a, b_a, w_i, b_i, lam):
    bsz, r2 = proj.shape
    r = r2 // 2
    nb, cb = w_a.shape[0], w_a.shape[1]
    taps = conv_w.shape[0]
    blk = lambda c: (0, c)
    return pl.pallas_call(
        _lru_sample_kernel,
        grid=(nb,),
        in_specs=[pl.BlockSpec((bsz, cb), blk),
                  pl.BlockSpec((bsz, cb), lambda c: (0, nb + c)),
                  pl.BlockSpec((bsz, cb), blk), pl.BlockSpec((bsz, cb), blk),
                  pl.BlockSpec((bsz, cb), blk), pl.BlockSpec((bsz, cb), blk),
                  pl.BlockSpec((taps, cb), blk), pl.BlockSpec((1, cb), blk),
                  pl.BlockSpec((1, cb, cb), lambda c: (c, 0, 0)), pl.BlockSpec((1, cb), blk),
                  pl.BlockSpec((1, cb, cb), lambda c: (c, 0, 0)), pl.BlockSpec((1, cb), blk),
                  pl.BlockSpec((1, cb), blk)],
        out_specs=[pl.BlockSpec((bsz, cb), blk), pl.BlockSpec((bsz, cb), blk)],
        out_shape=[jax.ShapeDtypeStruct((bsz, r), BF16), jax.ShapeDtypeStruct((bsz, r), F32)],
        compiler_params=_params(1),
        name="lru_sample",
    )(proj, proj, conv_rows[0], conv_rows[1], conv_rows[2], h0, conv_w, conv_b.reshape(1, r),
      w_a, b_a.reshape(1, r), w_i, b_i.reshape(1, r), lam.reshape(1, r))


def _pool_prompt_kernel(xn_ref, x_ref, w_ref, b_ref, sc_ref, o_ref, st_ref, ext_ref, wbf_ref,
                        *, tiles_per_seq, windows):
    tm, d = xn_ref.shape
    gd = d // len(windows)
    halo = 2 * SUBLANES
    i = pl.program_id(0)

    @pl.when(i == 0)
    def _():
        wbf_ref[...] = w_ref[...].astype(BF16)

    @pl.when(i % tiles_per_seq == 0)
    def _():
        ext_ref[0:halo, :] = jnp.zeros((halo, d), F32)

    ext_ref[halo:halo + tm, :] = xn_ref[...]
    pos = (i % tiles_per_seq) * tm + lax.broadcasted_iota(jnp.int32, (tm, 1), 0)
    for g, w in enumerate(windows):
        sl = slice(g * gd, (g + 1) * gd)
        s = ext_ref[:, sl]
        shift = 1
        while shift < w:
            s = s + pltpu.roll(s, shift, 0)
            shift *= 2
        cnt = jnp.minimum(pos + 1, w).astype(F32)
        pooled = s[halo:, :] / cnt
        diff = (pooled - xn_ref[:, sl]).astype(BF16)
        mixed = jnp.dot(diff, wbf_ref[g], preferred_element_type=F32) + b_ref[:, sl]
        o_ref[:, sl] = x_ref[:, sl] + mixed * sc_ref[:, sl]
    st_ref[0] = ext_ref[halo + tm - POOL_BUF:halo + tm, :]
    ext_ref[0:halo, :] = ext_ref[tm:tm + halo, :]


def _pool_prompt(xn, x, w, b, scale, n_seq, seq, tm=512):
    rows, d = xn.shape
    tm = min(tm, seq)
    tps = seq // tm
    ng, gd = w.shape[0], w.shape[1]
    kern = functools.partial(_pool_prompt_kernel, tiles_per_seq=tps, windows=POOL_WINDOWS)
    return pl.pallas_call(
        kern,
        grid=(rows // tm,),
        in_specs=[pl.BlockSpec((tm, d), lambda i: (i, 0)),
                  pl.BlockSpec((tm, d), lambda i: (i, 0)),
                  pl.BlockSpec((ng, gd, gd), lambda i: (0, 0, 0)),
                  pl.BlockSpec((1, d), lambda i: (0, 0)),
                  pl.BlockSpec((1, d), lambda i: (0, 0))],
        out_specs=[pl.BlockSpec((tm, d), lambda i: (i, 0)),
                   pl.BlockSpec((1, POOL_BUF, d), lambda i: (i // tps, 0, 0))],
        out_shape=[jax.ShapeDtypeStruct((rows, d), F32),
                   jax.ShapeDtypeStruct((n_seq, POOL_BUF, d), F32)],
        scratch_shapes=[pltpu.VMEM((tm + 2 * SUBLANES, d), F32),
                        pltpu.VMEM((ng, gd, gd), BF16)],
        compiler_params=_params(1),
        name="pool_prompt",
    )(xn, x, w, b.reshape(1, d), scale.reshape(1, d))


def _pool_sample_kernel(xn_ref, x_ref, st_ref, w_ref, b_ref, sc_ref, o_ref, *, window, pos0):
    xn = xn_ref[...]
    s = xn
    for r in range(POOL_BUF - (window - 1), POOL_BUF):
        s = s + st_ref[r]
    pooled = s / float(min(pos0 + 1, window))
    diff = (pooled - xn).astype(BF16)
    mixed = jnp.dot(diff, w_ref[0].astype(BF16), preferred_element_type=F32) + b_ref[...]
    o_ref[...] = x_ref[...] + mixed * sc_ref[...]


def _pool_sample(xn, x, state_t, w, b, scale, pos0):
    bsz, d = xn.shape
    ng, gd = w.shape[0], w.shape[1]
    outs = []
    for g, window in enumerate(POOL_WINDOWS):
        kern = functools.partial(_pool_sample_kernel, window=window, pos0=pos0)
        blk = lambda i, g=g: (0, g)
        outs.append(pl.pallas_call(
            kern,
            grid=(1,),
            in_specs=[pl.BlockSpec((bsz, gd), blk), pl.BlockSpec((bsz, gd), blk),
                      pl.BlockSpec((POOL_BUF, bsz, gd), lambda i, g=g: (0, 0, g)),
                      pl.BlockSpec((1, gd, gd), lambda i, g=g: (g, 0, 0)),
                      pl.BlockSpec((1, gd), blk), pl.BlockSpec((1, gd), blk)],
            out_specs=pl.BlockSpec((bsz, gd), lambda i: (0, 0)),
            out_shape=jax.ShapeDtypeStruct((bsz, gd), F32),
            compiler_params=_params(1),
            name="pool_sample",
        )(xn, x, state_t, w, b.reshape(1, d), scale.reshape(1, d)))
    return jnp.concatenate(outs, axis=-1)


def _s5_matrices(a_re, a_im, b_re, b_im, c_re, c_im, log_dt, chunk):
    g_all, p = a_re.shape
    c = b_re.shape[-1]
    gpb = S5_GROUPS_PER_BLOCK
    nb = g_all // gpb
    dt = jnp.exp(log_dt.astype(F32))[:, None]
    mag = jnp.exp(a_re * dt)
    lr, li = mag * jnp.cos(a_im * dt), mag * jnp.sin(a_im * dt)
    den = a_re * a_re + a_im * a_im
    q_re = ((lr - 1.0) * a_re + li * a_im) / den
    q_im = (li * a_re - (lr - 1.0) * a_im) / den
    bb_re = q_re[..., None] * b_re - q_im[..., None] * b_im
    bb_im = q_re[..., None] * b_im + q_im[..., None] * b_re

    def cmul(xr, xi, yr, yi):
        return xr * yr - xi * yi, xr * yi + xi * yr

    pw = [(jnp.ones_like(lr), jnp.zeros_like(lr))]
    for _ in range(chunk):
        pw.append(cmul(pw[-1][0], pw[-1][1], lr, li))
    hi = lax.Precision.HIGHEST
    lane_g = jnp.arange(LANES) // c
    state_g = jnp.arange(gpb * p) // p
    mask_cc = (lane_g[:, None] == lane_g[None, :]).astype(F32)
    mask_cp = (lane_g[:, None] == state_g[None, :]).astype(F32)
    spread = jnp.tile(jnp.eye(p, dtype=F32), (1, gpb))

    def lanes_by_state(x):
        return x.transpose(0, 1, 3, 2).reshape(x.shape[0], nb, LANES, p)

    def state_by_lanes(x):
        return x.transpose(0, 3, 1, 2).reshape(x.shape[0], p, nb, LANES).transpose(0, 2, 1, 3)

    pr_all = jnp.stack([x[0] for x in pw])
    pi_all = jnp.stack([x[1] for x in pw])
    wr, wi = cmul(c_re[None], c_im[None], pr_all[:, :, None, :], pi_all[:, :, None, :])
    wr, wi = state_by_lanes(wr), state_by_lanes(wi)

    bb_rows = jnp.concatenate([lanes_by_state(bb_re[None])[0], -lanes_by_state(bb_im[None])[0]],
                              axis=-1)
    w_cols = jnp.concatenate([wr[:chunk], wi[:chunk]], axis=2)
    blocks = mask_cc * jnp.einsum('brk,tbkc->tbrc', bb_rows, w_cols, precision=hi)
    zero = jnp.zeros_like(blocks[0])
    m_in = jnp.stack([jnp.concatenate([blocks[t - s] if t >= s else zero for t in range(chunk)],
                                      axis=-1) for s in range(chunk)], axis=1)
    m_in = m_in.reshape(nb, chunk * LANES, chunk * LANES).astype(BF16)

    pr_rev, pi_rev = pr_all[:chunk][::-1], pi_all[:chunk][::-1]
    er, ei = cmul(pr_rev[..., None], pi_rev[..., None], bb_re[None], bb_im[None])
    halves = [mask_cp * jnp.einsum('tbrp,pq->tbrq', lanes_by_state(e), spread, precision=hi)
              for e in (er, ei)]
    m_state = jnp.concatenate(halves, axis=-1).transpose(1, 0, 2, 3)
    m_state = m_state.reshape(nb, chunk * LANES, 2 * gpb * p).astype(BF16)

    halves = [mask_cp.T * jnp.einsum('qp,tbpc->tbqc', spread.T, w, precision=hi)
              for w in (wr[1:], -wi[1:])]
    m_out = jnp.concatenate(halves, axis=2).transpose(1, 2, 0, 3)
    m_out = m_out.reshape(nb, 2 * gpb * p, chunk * LANES).astype(BF16)

    qr, qi = pw[chunk]
    rows = [(jnp.ones_like(qr), jnp.zeros_like(qr))]
    for _ in range(SUBLANES):
        rows.append(cmul(rows[-1][0], rows[-1][1], qr, qi))
    table = jnp.stack([_s5_state_layout(r_, i_) for r_, i_ in rows])
    table = jnp.concatenate([table, jnp.zeros((2 * SUBLANES - table.shape[0], table.shape[1]), F32)])
    return m_in, m_state, m_out, table


def _s5_state_layout(re, im):
    lead = re.shape[:-2]
    g_all, p = re.shape[-2:]
    nb = g_all // S5_GROUPS_PER_BLOCK
    st = jnp.stack([re.reshape(lead + (nb, S5_GROUPS_PER_BLOCK * p)),
                    im.reshape(lead + (nb, S5_GROUPS_PER_BLOCK * p))], axis=-2)
    return st.reshape(lead + (2 * g_all * p,))


def _s5_state_unlayout(flat, g_all, p):
    lead = flat.shape[:-1]
    nb = g_all // S5_GROUPS_PER_BLOCK
    st = flat.reshape(lead + (nb, 2, S5_GROUPS_PER_BLOCK * p))
    return st[..., 0, :].reshape(lead + (g_all, p)), st[..., 1, :].reshape(lead + (g_all, p))


def _s5_lhs(u_refs):
    return jnp.concatenate([u[...] for u in u_refs], axis=1).astype(BF16)


def _s5_state_in_kernel(*refs, chunk):
    u_refs, m_ref, e_ref = refs[:chunk], refs[chunk], refs[chunk + 1]
    e_ref[...] = jnp.dot(_s5_lhs(u_refs), m_ref[0], preferred_element_type=F32)


def _s5_state_in(xnv, m_state, chunk, d, tr=1024):
    r = xnv.shape[0]
    nb, _, sw = m_state.shape
    tr = min(tr, r)
    lb = d // LANES
    kern = functools.partial(_s5_state_in_kernel, chunk=chunk)
    return pl.pallas_call(
        kern,
        grid=(nb, r // tr),
        in_specs=[pl.BlockSpec((tr, LANES), lambda b, i, t=t: (i, t * lb + b)) for t in range(chunk)]
        + [pl.BlockSpec((1, chunk * LANES, sw), lambda b, i: (b, 0, 0))],
        out_specs=pl.BlockSpec((tr, sw), lambda b, i: (i, b)),
        out_shape=jax.ShapeDtypeStruct((r, nb * sw), F32),
        compiler_params=_params(2),
        name="s5_state_in",
    )(*([xnv] * chunk), m_state)


def _s5_scan_kernel(e_ref, tab_ref, h_ref, fin_ref):
    rps, sw = e_ref.shape
    half = sw // 2
    groups = rps // SUBLANES
    er = e_ref[:, :half].reshape(groups, SUBLANES, half)
    ei = e_ref[:, half:].reshape(groups, SUBLANES, half)
    row = lax.broadcasted_iota(jnp.int32, (groups, SUBLANES, half), 1)
    s = 1
    while s < SUBLANES:
        keep = row >= s
        pr, pi = tab_ref[s:s + 1, :half], tab_ref[s:s + 1, half:]
        sr = jnp.where(keep, pltpu.roll(er, s, 1), 0.0)
        si = jnp.where(keep, pltpu.roll(ei, s, 1), 0.0)
        er, ei = er + pr * sr - pi * si, ei + pr * si + pi * sr
        s *= 2
    xr = jnp.where(row >= 1, pltpu.roll(er, 1, 1), 0.0)
    xi = jnp.where(row >= 1, pltpu.roll(ei, 1, 1), 0.0)
    ar, ai = tab_ref[0:SUBLANES, :half], tab_ref[0:SUBLANES, half:]
    l8r, l8i = tab_ref[SUBLANES:SUBLANES + 1, :half], tab_ref[SUBLANES:SUBLANES + 1, half:]
    cr = jnp.zeros((1, half), F32)
    ci = jnp.zeros((1, half), F32)
    for g in range(groups):
        lo, hi = g * SUBLANES, (g + 1) * SUBLANES
        h_ref[lo:hi, :half] = ar * cr - ai * ci + xr[g]
        h_ref[lo:hi, half:] = ar * ci + ai * cr + xi[g]
        cr, ci = (l8r * cr - l8i * ci + er[g, SUBLANES - 1:SUBLANES],
                  l8r * ci + l8i * cr + ei[g, SUBLANES - 1:SUBLANES])
    fin_ref[0, :, :half] = cr
    fin_ref[0, :, half:] = ci


def _s5_scan(e, table, n_seq, sw):
    r, width = e.shape
    rps = r // n_seq
    return pl.pallas_call(
        _s5_scan_kernel,
        grid=(width // sw, n_seq),
        in_specs=[pl.BlockSpec((rps, sw), lambda c, b: (b, c)),
                  pl.BlockSpec((2 * SUBLANES, sw), lambda c, b: (0, c))],
        out_specs=[pl.BlockSpec((rps, sw), lambda c, b: (b, c)),
                   pl.BlockSpec((1, 1, sw), lambda c, b: (b, 0, c))],
        out_shape=[jax.ShapeDtypeStruct((r, width), F32),
                   jax.ShapeDtypeStruct((n_seq, 1, width), F32)],
        compiler_params=_params(2),
        name="s5_scan",
    )(e, table)


def _s5_step_kernel(e_ref, h0_ref, tab_ref, o_ref):
    half = e_ref.shape[1] // 2
    lr, li = tab_ref[1:2, :half], tab_ref[1:2, half:]
    hr, hi = h0_ref[:, :half], h0_ref[:, half:]
    o_ref[:, :half] = lr * hr - li * hi + e_ref[:, :half]
    o_ref[:, half:] = lr * hi + li * hr + e_ref[:, half:]


def _s5_step(e, h0, table, sw):
    bsz, width = e.shape
    return pl.pallas_call(
        _s5_step_kernel,
        grid=(width // sw,),
        in_specs=[pl.BlockSpec((bsz, sw), lambda c: (0, c)),
                  pl.BlockSpec((bsz, sw), lambda c: (0, c)),
                  pl.BlockSpec((2 * SUBLANES, sw), lambda c: (0, c))],
        out_specs=pl.BlockSpec((bsz, sw), lambda c: (0, c)),
        out_shape=jax.ShapeDtypeStruct((bsz, width), F32),
        compiler_params=_params(1),
        name="s5_step",
    )(e, h0, table)


def _s5_out_kernel(*refs, chunk):
    u_refs = refs[:chunk]
    h_ref, mi_ref, mo_ref = refs[chunk:chunk + 3]
    y_refs = refs[chunk + 3:]
    y = jnp.dot(_s5_lhs(u_refs), mi_ref[0], preferred_element_type=F32)
    y = y + jnp.dot(h_ref[...].astype(BF16), mo_ref[0], preferred_element_type=F32)
    for t, y_ref in enumerate(y_refs):
        y_ref[...] = y[:, t * LANES:(t + 1) * LANES]


def _s5_out(xnv, h, m_in, m_out, chunk, d, tr=1024):
    r = xnv.shape[0]
    nb, sw, _ = m_out.shape
    tr = min(tr, r)
    lb = d // LANES
    kern = functools.partial(_s5_out_kernel, chunk=chunk)
    return pl.pallas_call(
        kern,
        grid=(nb, r // tr),
        in_specs=[pl.BlockSpec((tr, LANES), lambda b, i, t=t: (i, t * lb + b)) for t in range(chunk)]
        + [pl.BlockSpec((tr, sw), lambda b, i: (i, b)),
           pl.BlockSpec((1, chunk * LANES, chunk * LANES), lambda b, i: (b, 0, 0)),
           pl.BlockSpec((1, sw, chunk * LANES), lambda b, i: (b, 0, 0))],
        out_specs=[pl.BlockSpec((tr, LANES), lambda b, i: (i, b)) for _ in range(chunk)],
        out_shape=[jax.ShapeDtypeStruct((r, d), F32) for _ in range(chunk)],
        compiler_params=_params(2),
        name="s5_out",
    )(*([xnv] * chunk), h, m_in, m_out)


def _s5_glu_lhs(y, xn, dvec):
    return jax.nn.gelu(y + dvec * xn)


def _s5_mixer(x, xn, h0_flat, n_seq, prm, w_glu, b_glu, layer, chunk):
    a_re, a_im, b_re, b_im, c_re, c_im, dvec, log_dt = prm
    rows, d = xn.shape
    m_in, m_state, m_out, table = _s5_matrices(a_re, a_im, b_re, b_im, c_re, c_im, log_dt, chunk)
    sw = m_state.shape[-1]
    xnv = xn.reshape(rows // chunk, chunk * d)
    e = _s5_state_in(xnv, m_state, chunk, d)
    if h0_flat is None:
        h, fin = _s5_scan(e, table, n_seq, sw)
        fin = fin.reshape(n_seq, -1)
    else:
        h = h0_flat
        fin = _s5_step(e, h0_flat, table, sw)
    ys = _s5_out(xnv, h, m_in, m_out, chunk, d)
    y = jnp.stack(ys, axis=1).reshape(rows, d)
    b_glu3 = b_glu.reshape(b_glu.shape[0], 1, 2 * d)
    tiles = dict(tm=512, tn=1024, single_w=True) if h0_flat is None else {}
    x_new = _mm(y, [(w_glu, layer, 0), (w_glu, layer, d)], d, _ep_glu_residual, F32,
                lhs_fn=_s5_glu_lhs, lhs_extra=[xn, dvec.reshape(1, d)],
                rows_extra=[(b_glu3, layer, 0), (b_glu3, layer, d)],
                full_extra=[x], name="s5_glu", **tiles)
    return x_new, fin


def kernel(x_prompt, x_sample, cache_mem_k, cache_mem_v, state_lru_conv, state_lru_h, state_pool, state_s5_re, state_s5_im, mem_prompt, g_mix, g_xattn, g_mem, g_mlp, g_final, w_q, w_k, w_v, w_o, w_up, w_down, lru_w_in, lru_conv_w, lru_conv_b, lru_w_a, lru_b_a, lru_w_i, lru_b_i, lru_lambda, lru_w_o, pool_w, pool_b, pool_scale, s5_a_re, s5_a_im, s5_b_re, s5_b_im, s5_c_re, s5_c_im, s5_d, s5_log_dt, s5_w_glu, s5_b_glu):
    n_seq, seq, d = x_prompt.shape
    bsz, dec_seq, _ = x_sample.shape
    assert dec_seq == 1, "the sample group advances one token per request"
    depth = g_mix.shape[0]
    mem_tokens = mem_prompt.shape[1]
    heads, head_dim = cache_mem_k.shape[3], cache_mem_k.shape[4]
    g_all, p_state = s5_a_re.shape[1], s5_a_re.shape[2]
    past_len = 16384

    xp = x_prompt.reshape(n_seq * seq, d)
    xs = x_sample.reshape(bsz, d)
    mem = mem_prompt.reshape(n_seq * mem_tokens, d)
    wide = dict(tm=512, tn=d, single_w=True)
    assert SUBLANES == 2 * heads and head_dim % (2 * LANES) == 0, "slab view needs 4 heads"
    cache_k_view = _slab_view(cache_mem_k, heads)
    cache_v_view = _slab_view(cache_mem_v, heads)
    g_mix3, g_xattn3, g_mem3, g_mlp3 = (g.reshape(depth, 1, d)
                                        for g in (g_mix, g_xattn, g_mem, g_mlp))

    mem_k, mem_v = [], []
    lru_conv_p, lru_h_p, lru_conv_s, lru_h_s = [], [], [], []
    pool_p, pool_s = [], []
    s5_re_p, s5_im_p, s5_re_s, s5_im_s = [], [], [], []

    for i in range(depth):
        kind, j = i % 3, i // 3
        if kind == 0:
            prm = (lru_conv_w[j], lru_conv_b[j], lru_w_a[j], lru_b_a[j], lru_w_i[j], lru_b_i[j],
                   lru_lambda[j])
            proj = _mm(xp, [(lru_w_in, j, 0)], 2 * d, _ep_plain, F32, norm_g=(g_mix3, i),
                       name="lru_in", **wide)
            y, cst, hl = _lru_prompt(proj, *prm, n_seq, seq)
            xp = _mm(y, [(lru_w_o, j, 0)], d, _ep_residual, F32, full_extra=[xp], name="lru_out",
                     **wide)
            lru_conv_p.append(cst)
            lru_h_p.append(hl)

            proj = _mm(xs, [(lru_w_in, j, 0)], 2 * d, _ep_plain, F32, norm_g=(g_mix3, i),
                       name="lru_in_s")
            conv_rows = [state_lru_conv[j][:, r, :] for r in range(state_lru_conv.shape[2])]
            y, h_new = _lru_sample(proj, conv_rows, state_lru_h[j], *prm)
            xs = _mm(y, [(lru_w_o, j, 0)], d, _ep_residual, F32, full_extra=[xs], name="lru_out_s")
            lru_conv_s.append(jnp.concatenate(
                [state_lru_conv[j][:, 1:, :], proj[:, None, :d]], axis=1))
            lru_h_s.append(h_new)
        elif kind == 1:
            xn = _rmsnorm(xp, (g_mix3, i), F32)
            xp, st = _pool_prompt(xn, xp, pool_w[j], pool_b[j], pool_scale[j], n_seq, seq)
            pool_p.append(st)

            xn = _rmsnorm(xs, (g_mix3, i), F32)
            xs = _pool_sample(xn, xs, jnp.swapaxes(state_pool[j], 0, 1), pool_w[j], pool_b[j],
                              pool_scale[j], past_len)
            pool_s.append(jnp.concatenate([state_pool[j][:, 1:, :], xn[:, None, :]], axis=1))
        else:
            prm = (s5_a_re[j], s5_a_im[j], s5_b_re[j], s5_b_im[j], s5_c_re[j], s5_c_im[j],
                   s5_d[j], s5_log_dt[j])
            xp, fin = _s5_mixer(xp, _rmsnorm(xp, (g_mix3, i), F32), None, n_seq, prm,
                                s5_w_glu, s5_b_glu, j, S5_CHUNK)
            re, im = _s5_state_unlayout(fin, g_all, p_state)
            s5_re_p.append(re)
            s5_im_p.append(im)

            h0 = _s5_state_layout(state_s5_re[j], state_s5_im[j])
            xs, fin = _s5_mixer(xs, _rmsnorm(xs, (g_mix3, i), F32), h0, bsz, prm,
                                s5_w_glu, s5_b_glu, j, 1)
            re, im = _s5_state_unlayout(fin, g_all, p_state)
            s5_re_s.append(re)
            s5_im_s.append(im)

        k = _mm(mem, [(w_k, i, 0)], d, _ep_plain, F32, norm_g=(g_mem3, i), name="mem_k")
        v = _mm(mem, [(w_v, i, 0)], d, _ep_plain, F32, norm_g=(g_mem3, i), name="mem_v")
        mem_k.append(k.reshape(n_seq, mem_tokens, heads, head_dim))
        mem_v.append(v.reshape(n_seq, mem_tokens, heads, head_dim))
        q = _mm(xp, [(w_q, i, 0)], d, _ep_plain, BF16, norm_g=(g_xattn3, i), name="q", **wide)
        o = _attn_prompt(q, k.reshape(n_seq, mem_tokens, d), v.reshape(n_seq, mem_tokens, d),
                         n_seq, seq, heads)
        xp = _mm(o, [(w_o, i, 0)], d, _ep_residual, F32, full_extra=[xp], name="attn_out", **wide)

        q = _mm(xs, [(w_q, i, 0)], d, _ep_plain, F32, norm_g=(g_xattn3, i), name="q_s")
        o = _attn_sample(_slab_view(q.reshape(bsz, heads, head_dim), heads), cache_k_view,
                         cache_v_view, i, heads)
        xs = _mm(_slab_unview(o, heads).reshape(bsz, d), [(w_o, i, 0)], d, _ep_residual, F32,
                 full_extra=[xs], name="attn_out_s")

        xs, wu_bf, wd_bf = _mlp_cast(xs, (g_mlp3, i), w_up, w_down, i)
        xp = _mlp(xp, (g_mlp3, i), wu_bf, wd_bf)

    g_final3 = g_final.reshape(1, 1, d)
    y_prompt = _rmsnorm(xp, (g_final3, 0), F32).reshape(n_seq, seq, d)
    y_sample = _rmsnorm(xs, (g_final3, 0), F32).reshape(bsz, 1, d)
    return (y_prompt, y_sample,
            jnp.stack(mem_k), jnp.stack(mem_v),
            jnp.stack(lru_conv_p), jnp.stack(lru_h_p), jnp.stack(pool_p),
            jnp.stack(s5_re_p), jnp.stack(s5_im_p),
            jnp.stack(lru_conv_s), jnp.stack(lru_h_s), jnp.stack(pool_s),
            jnp.stack(s5_re_s), jnp.stack(s5_im_s))
```

```python
import functools
import math

import jax
import jax.numpy as jnp
from jax import lax
from jax.experimental import pallas as pl
from jax.experimental.pallas import tpu as pltpu

F32 = jnp.float32
BF16 = jnp.bfloat16

SUBLANES = 8
LANES = 128
VMEM_LIMIT_BYTES = 56 * 1024 * 1024

RMS_EPS = 1e-6
LRU_C = 8.0
POOL_WINDOWS = (2, 4, 8, 16)
POOL_BUF = max(POOL_WINDOWS) - 1
S5_GROUP_DIM = 16
S5_CHUNK = 8
S5_GROUPS_PER_BLOCK = LANES // S5_GROUP_DIM


def _params(n_axes):
    return pltpu.CompilerParams(dimension_semantics=("arbitrary",) * n_axes,
                                vmem_limit_bytes=VMEM_LIMIT_BYTES)


def _rms(x, g):
    xf = x.astype(F32)
    inv = lax.rsqrt(jnp.mean(xf * xf, axis=-1, keepdims=True) + RMS_EPS)
    return xf * inv * g.astype(F32)


def _rmsnorm_kernel(x_ref, g_ref, o_ref):
    o_ref[...] = _rms(x_ref[...], g_ref[0]).astype(o_ref.dtype)


def _rmsnorm(x, gain, out_dtype, tm=512):
    rows, d = x.shape
    tm = min(tm, rows)
    g, layer = gain
    return pl.pallas_call(
        _rmsnorm_kernel,
        grid=(rows // tm,),
        in_specs=[pl.BlockSpec((tm, d), lambda i: (i, 0)),
                  pl.BlockSpec((1, 1, d), lambda i: (layer, 0, 0))],
        out_specs=pl.BlockSpec((tm, d), lambda i: (i, 0)),
        out_shape=jax.ShapeDtypeStruct((rows, d), out_dtype),
        compiler_params=_params(1),
        name="rmsnorm",
    )(x, g)


def _mm_kernel(*refs, n_w, n_row, n_full, has_norm, n_lhs, lhs_fn, epilogue):
    x_ref = refs[0]
    refs = refs[1:]
    g_ref = refs[0] if has_norm else None
    refs = refs[1:] if has_norm else refs
    lhs_refs = refs[:n_lhs]
    refs = refs[n_lhs:]
    w_refs = refs[:n_w]
    row_refs = refs[n_w:n_w + n_row]
    full_refs = refs[n_w + n_row:n_w + n_row + n_full]
    o_ref = refs[n_w + n_row + n_full]
    wbf_refs = refs[1 + n_w + n_row + n_full:]

    @pl.when(pl.program_id(1) == 0)
    def _():
        for w_ref, wbf_ref in zip(w_refs, wbf_refs):
            wbf_ref[...] = w_ref[0].astype(BF16)

    if has_norm:
        xb = _rms(x_ref[...], g_ref[0]).astype(BF16)
    elif lhs_fn is not None:
        xb = lhs_fn(x_ref[...], *[r[...] for r in lhs_refs]).astype(BF16)
    else:
        xb = x_ref[...].astype(BF16)
    accs = [jnp.dot(xb, wbf_ref[...], preferred_element_type=F32) for wbf_ref in wbf_refs]
    out = epilogue(accs, [r[0] for r in row_refs], [f[...] for f in full_refs])
    o_ref[...] = out.astype(o_ref.dtype)


def _mm(x, ws, n_cols, epilogue, out_dtype, *, norm_g=None, lhs_fn=None, lhs_extra=(),
        rows_extra=(), full_extra=(), tm=1024, tn=512, single_w=False, name="mm"):
    rows, k = x.shape
    tm = min(tm, rows)
    tn = min(tn, n_cols)
    in_specs = [pl.BlockSpec((tm, k), lambda j, i: (i, 0))]
    args = [x]
    if norm_g is not None:
        g, g_layer = norm_g
        in_specs.append(pl.BlockSpec((1, 1, k), lambda j, i: (g_layer, 0, 0)))
        args.append(g)
    for e in lhs_extra:
        if e.shape[0] == 1:
            in_specs.append(pl.BlockSpec((1, k), lambda j, i: (0, 0)))
        else:
            in_specs.append(pl.BlockSpec((tm, k), lambda j, i: (i, 0)))
        args.append(e)
    w_mode = dict(pipeline_mode=pl.Buffered(1)) if single_w else {}
    for w, layer, off in ws:
        in_specs.append(pl.BlockSpec((1, k, tn), lambda j, i, l=layer, o=off // tn: (l, 0, o + j),
                                     **w_mode))
        args.append(w)
    for v, layer, off in rows_extra:
        in_specs.append(pl.BlockSpec((1, 1, tn), lambda j, i, l=layer, o=off // tn: (l, 0, o + j)))
        args.append(v)
    for f in full_extra:
        in_specs.append(pl.BlockSpec((tm, tn), lambda j, i: (i, j)))
        args.append(f)
    kern = functools.partial(_mm_kernel, n_w=len(ws), n_row=len(rows_extra),
                             n_full=len(full_extra), has_norm=norm_g is not None,
                             n_lhs=len(lhs_extra), lhs_fn=lhs_fn, epilogue=epilogue)
    return pl.pallas_call(
        kern,
        grid=(n_cols // tn, rows // tm),
        in_specs=in_specs,
        out_specs=pl.BlockSpec((tm, tn), lambda j, i: (i, j)),
        out_shape=jax.ShapeDtypeStruct((rows, n_cols), out_dtype),
        scratch_shapes=[pltpu.VMEM((k, tn), BF16) for _ in ws],
        compiler_params=_params(2),
        name=name,
    )(*args)


def _ep_plain(accs, rows, fulls):
    return accs[0]


def _ep_residual(accs, rows, fulls):
    return fulls[0] + accs[0]


def _ep_glu_residual(accs, rows, fulls):
    a = accs[0] + rows[0]
    b = accs[1] + rows[1]
    return fulls[0] + a * jax.nn.sigmoid(b)


def _mlp_kernel(x_ref, g_ref, wu_ref, wd_ref, o_ref, xn_ref):
    @pl.when(pl.program_id(1) == 0)
    def _():
        xv = x_ref[...]
        xn_ref[...] = _rms(xv, g_ref[0]).astype(BF16)
        o_ref[...] = xv

    h = jnp.dot(xn_ref[...], wu_ref[...], preferred_element_type=F32)
    h = jnp.maximum(h, 0.0)
    o_ref[...] += jnp.dot((h * h).astype(BF16), wd_ref[...], preferred_element_type=F32)


def _mlp(x, gain, wu_bf, wd_bf, tm=1024, tf=512):
    rows, d = x.shape
    dff = wu_bf.shape[1]
    tm = min(tm, rows)
    g, layer = gain
    return pl.pallas_call(
        _mlp_kernel,
        grid=(rows // tm, dff // tf),
        in_specs=[pl.BlockSpec((tm, d), lambda i, f: (i, 0)),
                  pl.BlockSpec((1, 1, d), lambda i, f: (layer, 0, 0)),
                  pl.BlockSpec((d, tf), lambda i, f: (0, f)),
                  pl.BlockSpec((tf, d), lambda i, f: (f, 0))],
        out_specs=pl.BlockSpec((tm, d), lambda i, f: (i, 0)),
        out_shape=jax.ShapeDtypeStruct((rows, d), F32),
        scratch_shapes=[pltpu.VMEM((tm, d), BF16)],
        compiler_params=_params(2),
        name="mlp",
    )(x, g, wu_bf, wd_bf)


def _mlp_cast_kernel(x_ref, g_ref, wu_ref, wd_ref, o_ref, wub_ref, wdb_ref, xn_ref):
    @pl.when(pl.program_id(0) == 0)
    def _():
        xv = x_ref[...]
        xn_ref[...] = _rms(xv, g_ref[0]).astype(BF16)
        o_ref[...] = xv

    wu = wu_ref[0].astype(BF16)
    wd = wd_ref[0].astype(BF16)
    wub_ref[...] = wu
    wdb_ref[...] = wd
    h = jnp.dot(xn_ref[...], wu, preferred_element_type=F32)
    h = jnp.maximum(h, 0.0)
    o_ref[...] += jnp.dot((h * h).astype(BF16), wd, preferred_element_type=F32)


def _mlp_cast(x, gain, w_up, w_down, layer, tf=512):
    rows, d = x.shape
    dff = w_up.shape[2]
    g, g_layer = gain
    return pl.pallas_call(
        _mlp_cast_kernel,
        grid=(dff // tf,),
        in_specs=[pl.BlockSpec((rows, d), lambda f: (0, 0)),
                  pl.BlockSpec((1, 1, d), lambda f: (g_layer, 0, 0)),
                  pl.BlockSpec((1, d, tf), lambda f: (layer, 0, f)),
                  pl.BlockSpec((1, tf, d), lambda f: (layer, f, 0))],
        out_specs=[pl.BlockSpec((rows, d), lambda f: (0, 0)),
                   pl.BlockSpec((d, tf), lambda f: (0, f)),
                   pl.BlockSpec((tf, d), lambda f: (f, 0))],
        out_shape=[jax.ShapeDtypeStruct((rows, d), F32),
                   jax.ShapeDtypeStruct((d, dff), BF16),
                   jax.ShapeDtypeStruct((dff, d), BF16)],
        scratch_shapes=[pltpu.VMEM((rows, d), BF16)],
        compiler_params=_params(1),
        name="mlp_cast",
    )(x, g, w_up, w_down)


def _attn_prompt_kernel(q_ref, k_ref, v_ref, o_ref, *, scale):
    q = q_ref[...]
    k = k_ref[0].astype(BF16)
    s = lax.dot_general(q, k, (((1,), (1,)), ((), ())), preferred_element_type=F32) * scale
    m = jnp.max(s, axis=-1, keepdims=True)
    e = jnp.exp(s - m)
    p = e / jnp.sum(e, axis=-1, keepdims=True)
    o = jnp.dot(p.astype(BF16), v_ref[0].astype(BF16), preferred_element_type=F32)
    o_ref[...] = o.astype(o_ref.dtype)


def _attn_prompt(q, k, v, n_seq, seq, heads, tq=1024):
    rows, d = q.shape
    m = k.shape[1]
    hd = d // heads
    tq = min(tq, seq)
    tps = seq // tq
    kern = functools.partial(_attn_prompt_kernel, scale=hd ** -0.5)
    return pl.pallas_call(
        kern,
        grid=(n_seq, tps, heads),
        in_specs=[pl.BlockSpec((tq, hd), lambda b, i, h: (b * tps + i, h)),
                  pl.BlockSpec((1, m, hd), lambda b, i, h: (b, 0, h)),
                  pl.BlockSpec((1, m, hd), lambda b, i, h: (b, 0, h))],
        out_specs=pl.BlockSpec((tq, hd), lambda b, i, h: (b * tps + i, h)),
        out_shape=jax.ShapeDtypeStruct((rows, d), BF16),
        compiler_params=_params(3),
        name="attn_prompt",
    )(q, k, v)


def _slab_view(x, heads):
    lead = x.shape[:-2]
    hd = x.shape[-1]
    n = len(lead)
    x = x.reshape(lead + (heads, hd // LANES, LANES))
    x = x.transpose(tuple(range(n)) + (n + 1, n, n + 2))
    return x.reshape(lead + (heads * hd // (SUBLANES * LANES), SUBLANES, LANES))


def _slab_unview(x, heads):
    lead = x.shape[:-3]
    n = len(lead)
    tiles = x.shape[-3] * SUBLANES // heads
    x = x.reshape(lead + (tiles, heads, LANES))
    x = x.transpose(tuple(range(n)) + (n + 1, n, n + 2))
    return x.reshape(lead + (heads, tiles * LANES))


def _attn_sample_kernel(q_ref, k_ref, v_ref, o_ref, *, heads, scale, bt):
    for b in range(bt):
        t = jnp.sum(k_ref[0, b] * q_ref[b][None], axis=1)
        t = t + pltpu.roll(t, heads, 1)
        s = jnp.sum(t, axis=-1, keepdims=True) * scale
        mx = jnp.max(s, axis=0, keepdims=True)
        e = jnp.exp(s - mx)
        p = e / jnp.sum(e, axis=0, keepdims=True)
        o_ref[b] = jnp.sum(p[:, None] * v_ref[0, b], axis=0)


def _attn_sample(q, cache_k, cache_v, layer, heads, bt=2):
    bsz, pairs = q.shape[:2]
    m = cache_k.shape[2]
    hd = pairs * SUBLANES * LANES // heads
    kern = functools.partial(_attn_sample_kernel, heads=heads, scale=hd ** -0.5, bt=bt)
    kv_spec = pl.BlockSpec((1, bt, m, pairs, SUBLANES, LANES), lambda i: (layer, i, 0, 0, 0, 0))
    q_spec = pl.BlockSpec((bt, pairs, SUBLANES, LANES), lambda i: (i, 0, 0, 0))
    return pl.pallas_call(
        kern,
        grid=(bsz // bt,),
        in_specs=[q_spec, kv_spec, kv_spec],
        out_specs=q_spec,
        out_shape=jax.ShapeDtypeStruct(q.shape, F32),
        compiler_params=_params(1),
        name="attn_sample",
    )(q, cache_k, cache_v)


def _lru_gates(xc, wa, ba, wi, bi, lam):
    xcb = xc.astype(BF16)
    r = jax.nn.sigmoid(jnp.dot(xcb, wa.astype(BF16), preferred_element_type=F32) + ba)
    ig = jax.nn.sigmoid(jnp.dot(xcb, wi.astype(BF16), preferred_element_type=F32) + bi)
    log_a = -LRU_C * r * jax.nn.softplus(-lam)
    a = jnp.exp(log_a)
    u = jnp.sqrt(-jnp.tanh(log_a) * (a * a + 1.0)) * (ig * xc)
    return a, u


def _lru_prompt_kernel(xb_ref, gt_ref, cw_ref, cb_ref, wa_ref, ba_ref, wi_ref, bi_ref, lam_ref,
                       y_ref, cst_ref, hl_ref, ext_ref, a_ref, u_ref, hs_ref, hc_ref,
                       *, tiles_per_seq, taps):
    tm, cb = xb_ref.shape
    halo = SUBLANES

    @pl.when(pl.program_id(1) % tiles_per_seq == 0)
    def _():
        ext_ref[0:halo, :] = jnp.zeros((halo, cb), F32)
        hc_ref[...] = jnp.zeros((1, cb), F32)

    ext_ref[halo:halo + tm, :] = xb_ref[...]
    cw = cw_ref[...]
    xc = cb_ref[...]
    for k in range(taps):
        start = halo - (taps - 1) + k
        xc = xc + cw[k:k + 1, :] * ext_ref[start:start + tm, :]
    cst_ref[0] = ext_ref[halo + tm - (taps - 1):halo + tm, :]
    ext_ref[0:halo, :] = ext_ref[tm:tm + halo, :]

    a, u = _lru_gates(xc, wa_ref[0], ba_ref[...], wi_ref[0], bi_ref[...], lam_ref[...])

    groups = tm // SUBLANES
    a = a.reshape(groups, SUBLANES, cb)
    u = u.reshape(groups, SUBLANES, cb)
    row = lax.broadcasted_iota(jnp.int32, (groups, SUBLANES, cb), 1)
    s = 1
    while s < SUBLANES:
        keep = row >= s
        u = u + a * jnp.where(keep, pltpu.roll(u, s, 1), 0.0)
        a = a * jnp.where(keep, pltpu.roll(a, s, 1), 1.0)
        s *= 2
    a_ref[...] = a.reshape(tm, cb)
    u_ref[...] = u.reshape(tm, cb)

    def link(g, h):
        off = pl.multiple_of(g * SUBLANES, SUBLANES)
        ag = a_ref[pl.ds(off, SUBLANES), :]
        ug = u_ref[pl.ds(off, SUBLANES), :]
        hs_ref[pl.ds(off, SUBLANES), :] = ag * h + ug
        return ag[SUBLANES - 1:SUBLANES, :] * h + ug[SUBLANES - 1:SUBLANES, :]

    h_last = lax.fori_loop(0, tm // SUBLANES, link, hc_ref[...], unroll=4)
    hc_ref[...] = h_last
    hl_ref[0] = h_last
    y_ref[...] = (hs_ref[...] * jax.nn.gelu(gt_ref[...])).astype(y_ref.dtype)


def _lru_prompt(proj, conv_w, conv_b, w_a, b_a, w_i, b_i, lam, n_seq, seq, tm=1024):
    rows, r2 = proj.shape
    r = r2 // 2
    nb, cb = w_a.shape[0], w_a.shape[1]
    taps = conv_w.shape[0]
    tm = min(tm, seq)
    tps = seq // tm
    kern = functools.partial(_lru_prompt_kernel, tiles_per_seq=tps, taps=taps)
    vec = lambda c, i: (0, c)
    y, cst, hl = pl.pallas_call(
        kern,
        grid=(nb, rows // tm),
        in_specs=[pl.BlockSpec((tm, cb), lambda c, i: (i, c)),
                  pl.BlockSpec((tm, cb), lambda c, i: (i, nb + c)),
                  pl.BlockSpec((taps, cb), vec),
                  pl.BlockSpec((1, cb), vec),
                  pl.BlockSpec((1, cb, cb), lambda c, i: (c, 0, 0)),
                  pl.BlockSpec((1, cb), vec),
                  pl.BlockSpec((1, cb, cb), lambda c, i: (c, 0, 0)),
                  pl.BlockSpec((1, cb), vec),
                  pl.BlockSpec((1, cb), vec)],
        out_specs=[pl.BlockSpec((tm, cb), lambda c, i: (i, c)),
                   pl.BlockSpec((1, taps - 1, cb), lambda c, i: (i // tps, 0, c)),
                   pl.BlockSpec((1, 1, cb), lambda c, i: (i // tps, 0, c))],
        out_shape=[jax.ShapeDtypeStruct((rows, r), BF16),
                   jax.ShapeDtypeStruct((n_seq, taps - 1, r), F32),
                   jax.ShapeDtypeStruct((n_seq, 1, r), F32)],
        scratch_shapes=[pltpu.VMEM((tm + SUBLANES, cb), F32),
                        pltpu.VMEM((tm, cb), F32),
                        pltpu.VMEM((tm, cb), F32),
                        pltpu.VMEM((tm, cb), F32),
                        pltpu.VMEM((1, cb), F32)],
        compiler_params=_params(2),
        name="lru_prompt",
    )(proj, proj, conv_w, conv_b.reshape(1, r), w_a, b_a.reshape(1, r), w_i, b_i.reshape(1, r),
      lam.reshape(1, r))
    return y, cst, hl.reshape(n_seq, r)


def _lru_sample_kernel(xb_ref, gt_ref, c0_ref, c1_ref, c2_ref, h0_ref, cw_ref, cb_ref,
                       wa_ref, ba_ref, wi_ref, bi_ref, lam_ref, y_ref, h_ref):
    cw = cw_ref[...]
    xc = cb_ref[...]
    for k, c_ref in enumerate((c0_ref, c1_ref, c2_ref, xb_ref)):
        xc = xc + cw[k:k + 1, :] * c_ref[...]
    a, u = _lru_gates(xc, wa_ref[0], ba_ref[...], wi_ref[0], bi_ref[...], lam_ref[...])
    h = a * h0_ref[...] + u
    h_ref[...] = h
    y_ref[...] = (h * jax.nn.gelu(gt_ref[...])).astype(y_ref.dtype)


def _lru_sample(proj, conv_rows, h0, conv_w, conv_b, w_a, b_a, w_i, b_i, lam):
    bsz, r2 = proj.shape
    r = r2 // 2
    nb, cb = w_a.shape[0], w_a.shape[1]
    taps = conv_w.shape[0]
    blk = lambda c: (0, c)
    return pl.pallas_call(
        _lru_sample_kernel,
        grid=(nb,),
        in_specs=[pl.BlockSpec((bsz, cb), blk),
                  pl.BlockSpec((bsz, cb), lambda c: (0, nb + c)),
                  pl.BlockSpec((bsz, cb), blk), pl.BlockSpec((bsz, cb), blk),
                  pl.BlockSpec((bsz, cb), blk), pl.BlockSpec((bsz, cb), blk),
                  pl.BlockSpec((taps, cb), blk), pl.BlockSpec((1, cb), blk),
                  pl.BlockSpec((1, cb, cb), lambda c: (c, 0, 0)), pl.BlockSpec((1, cb), blk),
                  pl.BlockSpec((1, cb, cb), lambda c: (c, 0, 0)), pl.BlockSpec((1, cb), blk),
                  pl.BlockSpec((1, cb), blk)],
        out_specs=[pl.BlockSpec((bsz, cb), blk), pl.BlockSpec((bsz, cb), blk)],
        out_shape=[jax.ShapeDtypeStruct((bsz, r), BF16), jax.ShapeDtypeStruct((bsz, r), F32)],
        compiler_params=_params(1),
        name="lru_sample",
    )(proj, proj, conv_rows[0], conv_rows[1], conv_rows[2], h0, conv_w, conv_b.reshape(1, r),
      w_a, b_a.reshape(1, r), w_i, b_i.reshape(1, r), lam.reshape(1, r))


def _pool_prompt_kernel(xn_ref, x_ref, w_ref, b_ref, sc_ref, o_ref, st_ref, ext_ref, wbf_ref,
                        *, tiles_per_seq, windows):
    tm, d = xn_ref.shape
    gd = d // len(windows)
    halo = 2 * SUBLANES
    i = pl.program_id(0)

    @pl.when(i == 0)
    def _():
        wbf_ref[...] = w_ref[...].astype(BF16)

    @pl.when(i % tiles_per_seq == 0)
    def _():
        ext_ref[0:halo, :] = jnp.zeros((halo, d), F32)

    ext_ref[halo:halo + tm, :] = xn_ref[...]
    pos = (i % tiles_per_seq) * tm + lax.broadcasted_iota(jnp.int32, (tm, 1), 0)
    for g, w in enumerate(windows):
        sl = slice(g * gd, (g + 1) * gd)
        s = ext_ref[:, sl]
        shift = 1
        while shift < w:
            s = s + pltpu.roll(s, shift, 0)
            shift *= 2
        cnt = jnp.minimum(pos + 1, w).astype(F32)
        pooled = s[halo:, :] / cnt
        diff = (pooled - xn_ref[:, sl]).astype(BF16)
        mixed = jnp.dot(diff, wbf_ref[g], preferred_element_type=F32) + b_ref[:, sl]
        o_ref[:, sl] = x_ref[:, sl] + mixed * sc_ref[:, sl]
    st_ref[0] = ext_ref[halo + tm - POOL_BUF:halo + tm, :]
    ext_ref[0:halo, :] = ext_ref[tm:tm + halo, :]


def _pool_prompt(xn, x, w, b, scale, n_seq, seq, tm=512):
    rows, d = xn.shape
    tm = min(tm, seq)
    tps = seq // tm
    ng, gd = w.shape[0], w.shape[1]
    kern = functools.partial(_pool_prompt_kernel, tiles_per_seq=tps, windows=POOL_WINDOWS)
    return pl.pallas_call(
        kern,
        grid=(rows // tm,),
        in_specs=[pl.BlockSpec((tm, d), lambda i: (i, 0)),
                  pl.BlockSpec((tm, d), lambda i: (i, 0)),
                  pl.BlockSpec((ng, gd, gd), lambda i: (0, 0, 0)),
                  pl.BlockSpec((1, d), lambda i: (0, 0)),
                  pl.BlockSpec((1, d), lambda i: (0, 0))],
        out_specs=[pl.BlockSpec((tm, d), lambda i: (i, 0)),
                   pl.BlockSpec((1, POOL_BUF, d), lambda i: (i // tps, 0, 0))],
        out_shape=[jax.ShapeDtypeStruct((rows, d), F32),
                   jax.ShapeDtypeStruct((n_seq, POOL_BUF, d), F32)],
        scratch_shapes=[pltpu.VMEM((tm + 2 * SUBLANES, d), F32),
                        pltpu.VMEM((ng, gd, gd), BF16)],
        compiler_params=_params(1),
        name="pool_prompt",
    )(xn, x, w, b.reshape(1, d), scale.reshape(1, d))


def _pool_sample_kernel(xn_ref, x_ref, st_ref, w_ref, b_ref, sc_ref, o_ref, *, window, pos0):
    xn = xn_ref[...]
    s = xn
    for r in range(POOL_BUF - (window - 1), POOL_BUF):
        s = s + st_ref[r]
    pooled = s / float(min(pos0 + 1, window))
    diff = (pooled - xn).astype(BF16)
    mixed = jnp.dot(diff, w_ref[0].astype(BF16), preferred_element_type=F32) + b_ref[...]
    o_ref[...] = x_ref[...] + mixed * sc_ref[...]


def _pool_sample(xn, x, state_t, w, b, scale, pos0):
    bsz, d = xn.shape
    ng, gd = w.shape[0], w.shape[1]
    outs = []
    for g, window in enumerate(POOL_WINDOWS):
        kern = functools.partial(_pool_sample_kernel, window=window, pos0=pos0)
        blk = lambda i, g=g: (0, g)
        outs.append(pl.pallas_call(
            kern,
            grid=(1,),
            in_specs=[pl.BlockSpec((bsz, gd), blk), pl.BlockSpec((bsz, gd), blk),
                      pl.BlockSpec((POOL_BUF, bsz, gd), lambda i, g=g: (0, 0, g)),
                      pl.BlockSpec((1, gd, gd), lambda i, g=g: (g, 0, 0)),
                      pl.BlockSpec((1, gd), blk), pl.BlockSpec((1, gd), blk)],
            out_specs=pl.BlockSpec((bsz, gd), lambda i: (0, 0)),
            out_shape=jax.ShapeDtypeStruct((bsz, gd), F32),
            compiler_params=_params(1),
            name="pool_sample",
        )(xn, x, state_t, w, b.reshape(1, d), scale.reshape(1, d)))
    return jnp.concatenate(outs, axis=-1)


def _s5_matrices(a_re, a_im, b_re, b_im, c_re, c_im, log_dt, chunk):
    g_all, p = a_re.shape
    c = b_re.shape[-1]
    gpb = S5_GROUPS_PER_BLOCK
    nb = g_all // gpb
    dt = jnp.exp(log_dt.astype(F32))[:, None]
    mag = jnp.exp(a_re * dt)
    lr, li = mag * jnp.cos(a_im * dt), mag * jnp.sin(a_im * dt)
    den = a_re * a_re + a_im * a_im
    q_re = ((lr - 1.0) * a_re + li * a_im) / den
    q_im = (li * a_re - (lr - 1.0) * a_im) / den
    bb_re = q_re[..., None] * b_re - q_im[..., None] * b_im
    bb_im = q_re[..., None] * b_im + q_im[..., None] * b_re

    def cmul(xr, xi, yr, yi):
        return xr * yr - xi * yi, xr * yi + xi * yr

    pw = [(jnp.ones_like(lr), jnp.zeros_like(lr))]
    for _ in range(chunk):
        pw.append(cmul(pw[-1][0], pw[-1][1], lr, li))
    hi = lax.Precision.HIGHEST
    lane_g = jnp.arange(LANES) // c
    state_g = jnp.arange(gpb * p) // p
    mask_cc = (lane_g[:, None] == lane_g[None, :]).astype(F32)
    mask_cp = (lane_g[:, None] == state_g[None, :]).astype(F32)
    spread = jnp.tile(jnp.eye(p, dtype=F32), (1, gpb))

    def lanes_by_state(x):
        return x.transpose(0, 1, 3, 2).reshape(x.shape[0], nb, LANES, p)

    def state_by_lanes(x):
        return x.transpose(0, 3, 1, 2).reshape(x.shape[0], p, nb, LANES).transpose(0, 2, 1, 3)

    pr_all = jnp.stack([x[0] for x in pw])
    pi_all = jnp.stack([x[1] for x in pw])
    wr, wi = cmul(c_re[None], c_im[None], pr_all[:, :, None, :], pi_all[:, :, None, :])
    wr, wi = state_by_lanes(wr), state_by_lanes(wi)

    bb_rows = jnp.concatenate([lanes_by_state(bb_re[None])[0], -lanes_by_state(bb_im[None])[0]],
                              axis=-1)
    w_cols = jnp.concatenate([wr[:chunk], wi[:chunk]], axis=2)
    blocks = mask_cc * jnp.einsum('brk,tbkc->tbrc', bb_rows, w_cols, precision=hi)
    zero = jnp.zeros_like(blocks[0])
    m_in = jnp.stack([jnp.concatenate([blocks[t - s] if t >= s else zero for t in range(chunk)],
                                      axis=-1) for s in range(chunk)], axis=1)
    m_in = m_in.reshape(nb, chunk * LANES, chunk * LANES).astype(BF16)

    pr_rev, pi_rev = pr_all[:chunk][::-1], pi_all[:chunk][::-1]
    er, ei = cmul(pr_rev[..., None], pi_rev[..., None], bb_re[None], bb_im[None])
    halves = [mask_cp * jnp.einsum('tbrp,pq->tbrq', lanes_by_state(e), spread, precision=hi)
              for e in (er, ei)]
    m_state = jnp.concatenate(halves, axis=-1).transpose(1, 0, 2, 3)
    m_state = m_state.reshape(nb, chunk * LANES, 2 * gpb * p).astype(BF16)

    halves = [mask_cp.T * jnp.einsum('qp,tbpc->tbqc', spread.T, w, precision=hi)
              for w in (wr[1:], -wi[1:])]
    m_out = jnp.concatenate(halves, axis=2).transpose(1, 2, 0, 3)
    m_out = m_out.reshape(nb, 2 * gpb * p, chunk * LANES).astype(BF16)

    qr, qi = pw[chunk]
    rows = [(jnp.ones_like(qr), jnp.zeros_like(qr))]
    for _ in range(SUBLANES):
        rows.append(cmul(rows[-1][0], rows[-1][1], qr, qi))
    table = jnp.stack([_s5_state_layout(r_, i_) for r_, i_ in rows])
    table = jnp.concatenate([table, jnp.zeros((2 * SUBLANES - table.shape[0], table.shape[1]), F32)])
    return m_in, m_state, m_out, table


def _s5_state_layout(re, im):
    lead = re.shape[:-2]
    g_all, p = re.shape[-2:]
    nb = g_all // S5_GROUPS_PER_BLOCK
    st = jnp.stack([re.reshape(lead + (nb, S5_GROUPS_PER_BLOCK * p)),
                    im.reshape(lead + (nb, S5_GROUPS_PER_BLOCK * p))], axis=-2)
    return st.reshape(lead + (2 * g_all * p,))


def _s5_state_unlayout(flat, g_all, p):
    lead = flat.shape[:-1]
    nb = g_all // S5_GROUPS_PER_BLOCK
    st = flat.reshape(lead + (nb, 2, S5_GROUPS_PER_BLOCK * p))
    return st[..., 0, :].reshape(lead + (g_all, p)), st[..., 1, :].reshape(lead + (g_all, p))


def _s5_chunk_rows(t, tr, chunk):
    return pl.ds(t, tr, stride=chunk) if chunk > 1 else pl.ds(0, tr)


def _s5_lhs(u_ref, chunk):
    tr = u_ref.shape[0] // chunk
    parts = [u_ref[_s5_chunk_rows(t, tr, chunk), :] for t in range(chunk)]
    return jnp.concatenate(parts, axis=1).astype(BF16)


def _s5_state_in_kernel(u_ref, m_ref, e_ref, *, chunk):
    e_ref[...] = jnp.dot(_s5_lhs(u_ref, chunk), m_ref[0], preferred_element_type=F32)


def _s5_state_in(xn, m_state, chunk, tr=1024):
    rows, d = xn.shape
    r = rows // chunk
    nb, _, sw = m_state.shape
    tr = min(tr, r)
    kern = functools.partial(_s5_state_in_kernel, chunk=chunk)
    return pl.pallas_call(
        kern,
        grid=(nb, r // tr),
        in_specs=[pl.BlockSpec((tr * chunk, LANES), lambda b, i: (i, b)),
                  pl.BlockSpec((1, chunk * LANES, sw), lambda b, i: (b, 0, 0))],
        out_specs=pl.BlockSpec((tr, sw), lambda b, i: (i, b)),
        out_shape=jax.ShapeDtypeStruct((r, nb * sw), F32),
        compiler_params=_params(2),
        name="s5_state_in",
    )(xn, m_state)


def _s5_scan_kernel(e_ref, tab_ref, h_ref, fin_ref):
    rps, sw = e_ref.shape
    half = sw // 2
    groups = rps // SUBLANES
    er = e_ref[:, :half].reshape(groups, SUBLANES, half)
    ei = e_ref[:, half:].reshape(groups, SUBLANES, half)
    row = lax.broadcasted_iota(jnp.int32, (groups, SUBLANES, half), 1)
    s = 1
    while s < SUBLANES:
        keep = row >= s
        pr, pi = tab_ref[s:s + 1, :half], tab_ref[s:s + 1, half:]
        sr = jnp.where(keep, pltpu.roll(er, s, 1), 0.0)
        si = jnp.where(keep, pltpu.roll(ei, s, 1), 0.0)
        er, ei = er + pr * sr - pi * si, ei + pr * si + pi * sr
        s *= 2
    xr = jnp.where(row >= 1, pltpu.roll(er, 1, 1), 0.0)
    xi = jnp.where(row >= 1, pltpu.roll(ei, 1, 1), 0.0)
    ar, ai = tab_ref[0:SUBLANES, :half], tab_ref[0:SUBLANES, half:]
    l8r, l8i = tab_ref[SUBLANES:SUBLANES + 1, :half], tab_ref[SUBLANES:SUBLANES + 1, half:]
    cr = jnp.zeros((1, half), F32)
    ci = jnp.zeros((1, half), F32)
    for g in range(groups):
        lo, hi = g * SUBLANES, (g + 1) * SUBLANES
        h_ref[lo:hi, :half] = ar * cr - ai * ci + xr[g]
        h_ref[lo:hi, half:] = ar * ci + ai * cr + xi[g]
        cr, ci = (l8r * cr - l8i * ci + er[g, SUBLANES - 1:SUBLANES],
                  l8r * ci + l8i * cr + ei[g, SUBLANES - 1:SUBLANES])
    fin_ref[0, :, :half] = cr
    fin_ref[0, :, half:] = ci


def _s5_scan(e, table, n_seq, sw):
    r, width = e.shape
    rps = r // n_seq
    return pl.pallas_call(
        _s5_scan_kernel,
        grid=(width // sw, n_seq),
        in_specs=[pl.BlockSpec((rps, sw), lambda c, b: (b, c)),
                  pl.BlockSpec((2 * SUBLANES, sw), lambda c, b: (0, c))],
        out_specs=[pl.BlockSpec((rps, sw), lambda c, b: (b, c)),
                   pl.BlockSpec((1, 1, sw), lambda c, b: (b, 0, c))],
        out_shape=[jax.ShapeDtypeStruct((r, width), F32),
                   jax.ShapeDtypeStruct((n_seq, 1, width), F32)],
        compiler_params=_params(2),
        name="s5_scan",
    )(e, table)


def _s5_step_kernel(e_ref, h0_ref, tab_ref, o_ref):
    half = e_ref.shape[1] // 2
    lr, li = tab_ref[1:2, :half], tab_ref[1:2, half:]
    hr, hi = h0_ref[:, :half], h0_ref[:, half:]
    o_ref[:, :half] = lr * hr - li * hi + e_ref[:, :half]
    o_ref[:, half:] = lr * hi + li * hr + e_ref[:, half:]


def _s5_step(e, h0, table, sw):
    bsz, width = e.shape
    return pl.pallas_call(
        _s5_step_kernel,
        grid=(width // sw,),
        in_specs=[pl.BlockSpec((bsz, sw), lambda c: (0, c)),
                  pl.BlockSpec((bsz, sw), lambda c: (0, c)),
                  pl.BlockSpec((2 * SUBLANES, sw), lambda c: (0, c))],
        out_specs=pl.BlockSpec((bsz, sw), lambda c: (0, c)),
        out_shape=jax.ShapeDtypeStruct((bsz, width), F32),
        compiler_params=_params(1),
        name="s5_step",
    )(e, h0, table)


def _s5_out_kernel(u_ref, h_ref, mi_ref, mo_ref, y_ref, *, chunk):
    tr = h_ref.shape[0]
    y = jnp.dot(_s5_lhs(u_ref, chunk), mi_ref[0], preferred_element_type=F32)
    y = y + jnp.dot(h_ref[...].astype(BF16), mo_ref[0], preferred_element_type=F32)
    for t in range(chunk):
        y_ref[_s5_chunk_rows(t, tr, chunk), :] = y[:, t * LANES:(t + 1) * LANES]


def _s5_out(xn, h, m_in, m_out, chunk, tr=1024):
    rows, d = xn.shape
    r = rows // chunk
    nb, sw, _ = m_out.shape
    tr = min(tr, r)
    kern = functools.partial(_s5_out_kernel, chunk=chunk)
    return pl.pallas_call(
        kern,
        grid=(nb, r // tr),
        in_specs=[pl.BlockSpec((tr * chunk, LANES), lambda b, i: (i, b)),
                  pl.BlockSpec((tr, sw), lambda b, i: (i, b)),
                  pl.BlockSpec((1, chunk * LANES, chunk * LANES), lambda b, i: (b, 0, 0)),
                  pl.BlockSpec((1, sw, chunk * LANES), lambda b, i: (b, 0, 0))],
        out_specs=pl.BlockSpec((tr * chunk, LANES), lambda b, i: (i, b)),
        out_shape=jax.ShapeDtypeStruct((rows, d), F32),
        compiler_params=_params(2),
        name="s5_out",
    )(xn, h, m_in, m_out)


def _s5_glu_lhs(y, xn, dvec):
    return jax.nn.gelu(y + dvec * xn)


def _s5_mixer(x, xn, h0_flat, n_seq, prm, w_glu, b_glu, layer, chunk):
    a_re, a_im, b_re, b_im, c_re, c_im, dvec, log_dt = prm
    rows, d = xn.shape
    m_in, m_state, m_out, table = _s5_matrices(a_re, a_im, b_re, b_im, c_re, c_im, log_dt, chunk)
    sw = m_state.shape[-1]
    e = _s5_state_in(xn, m_state, chunk)
    if h0_flat is None:
        h, fin = _s5_scan(e, table, n_seq, sw)
        fin = fin.reshape(n_seq, -1)
    else:
        h = h0_flat
        fin = _s5_step(e, h0_flat, table, sw)
    y = _s5_out(xn, h, m_in, m_out, chunk)
    b_glu3 = b_glu.reshape(b_glu.shape[0], 1, 2 * d)
    tiles = dict(tm=512, tn=1024, single_w=True) if h0_flat is None else {}
    x_new = _mm(y, [(w_glu, layer, 0), (w_glu, layer, d)], d, _ep_glu_residual, F32,
                lhs_fn=_s5_glu_lhs, lhs_extra=[xn, dvec.reshape(1, d)],
                rows_extra=[(b_glu3, layer, 0), (b_glu3, layer, d)],
                full_extra=[x], name="s5_glu", **tiles)
    return x_new, fin


def kernel(x_prompt, x_sample, cache_mem_k, cache_mem_v, state_lru_conv, state_lru_h, state_pool, state_s5_re, state_s5_im, mem_prompt, g_mix, g_xattn, g_mem, g_mlp, g_final, w_q, w_k, w_v, w_o, w_up, w_down, lru_w_in, lru_conv_w, lru_conv_b, lru_w_a, lru_b_a, lru_w_i, lru_b_i, lru_lambda, lru_w_o, pool_w, pool_b, pool_scale, s5_a_re, s5_a_im, s5_b_re, s5_b_im, s5_c_re, s5_c_im, s5_d, s5_log_dt, s5_w_glu, s5_b_glu):
    n_seq, seq, d = x_prompt.shape
    bsz, dec_seq, _ = x_sample.shape
    assert dec_seq == 1, "the sample group advances one token per request"
    depth = g_mix.shape[0]
    mem_tokens = mem_prompt.shape[1]
    heads, head_dim = cache_mem_k.shape[3], cache_mem_k.shape[4]
    g_all, p_state = s5_a_re.shape[1], s5_a_re.shape[2]
    past_len = 16384

    xp = x_prompt.reshape(n_seq * seq, d)
    xs = x_sample.reshape(bsz, d)
    mem = mem_prompt.reshape(n_seq * mem_tokens, d)
    wide = dict(tm=512, tn=d, single_w=True)
    assert SUBLANES == 2 * heads and head_dim % (2 * LANES) == 0, "slab view needs 4 heads"
    cache_k_view = _slab_view(cache_mem_k, heads)
    cache_v_view = _slab_view(cache_mem_v, heads)
    g_mix3, g_xattn3, g_mem3, g_mlp3 = (g.reshape(depth, 1, d)
                                        for g in (g_mix, g_xattn, g_mem, g_mlp))

    mem_k, mem_v = [], []
    lru_conv_p, lru_h_p, lru_conv_s, lru_h_s = [], [], [], []
    pool_p, pool_s = [], []
    s5_re_p, s5_im_p, s5_re_s, s5_im_s = [], [], [], []

    for i in range(depth):
        kind, j = i % 3, i // 3
        if kind == 0:
            prm = (lru_conv_w[j], lru_conv_b[j], lru_w_a[j], lru_b_a[j], lru_w_i[j], lru_b_i[j],
                   lru_lambda[j])
            proj = _mm(xp, [(lru_w_in, j, 0)], 2 * d, _ep_plain, F32, norm_g=(g_mix3, i),
                       name="lru_in", **wide)
            y, cst, hl = _lru_prompt(proj, *prm, n_seq, seq)
            xp = _mm(y, [(lru_w_o, j, 0)], d, _ep_residual, F32, full_extra=[xp], name="lru_out",
                     **wide)
            lru_conv_p.append(cst)
            lru_h_p.append(hl)

            proj = _mm(xs, [(lru_w_in, j, 0)], 2 * d, _ep_plain, F32, norm_g=(g_mix3, i),
                       name="lru_in_s")
            conv_rows = [state_lru_conv[j][:, r, :] for r in range(state_lru_conv.shape[2])]
            y, h_new = _lru_sample(proj, conv_rows, state_lru_h[j], *prm)
            xs = _mm(y, [(lru_w_o, j, 0)], d, _ep_residual, F32, full_extra=[xs], name="lru_out_s")
            lru_conv_s.append(jnp.concatenate(
                [state_lru_conv[j][:, 1:, :], proj[:, None, :d]], axis=1))
            lru_h_s.append(h_new)
        elif kind == 1:
            xn = _rmsnorm(xp, (g_mix3, i), F32)
            xp, st = _pool_prompt(xn, xp, pool_w[j], pool_b[j], pool_scale[j], n_seq, seq)
            pool_p.append(st)

            xn = _rmsnorm(xs, (g_mix3, i), F32)
            xs = _pool_sample(xn, xs, jnp.swapaxes(state_pool[j], 0, 1), pool_w[j], pool_b[j],
                              pool_scale[j], past_len)
            pool_s.append(jnp.concatenate([state_pool[j][:, 1:, :], xn[:, None, :]], axis=1))
        else:
            prm = (s5_a_re[j], s5_a_im[j], s5_b_re[j], s5_b_im[j], s5_c_re[j], s5_c_im[j],
                   s5_d[j], s5_log_dt[j])
            xp, fin = _s5_mixer(xp, _rmsnorm(xp, (g_mix3, i), F32), None, n_seq, prm,
                                s5_w_glu, s5_b_glu, j, S5_CHUNK)
            re, im = _s5_state_unlayout(fin, g_all, p_state)
            s5_re_p.append(re)
            s5_im_p.append(im)

            h0 = _s5_state_layout(state_s5_re[j], state_s5_im[j])
            xs, fin = _s5_mixer(xs, _rmsnorm(xs, (g_mix3, i), F32), h0, bsz, prm,
                                s5_w_glu, s5_b_glu, j, 1)
            re, im = _s5_state_unlayout(fin, g_all, p_state)
            s5_re_s.append(re)
            s5_im_s.append(im)

        k = _mm(mem, [(w_k, i, 0)], d, _ep_plain, F32, norm_g=(g_mem3, i), name="mem_k")
        v = _mm(mem, [(w_v, i, 0)], d, _ep_plain, F32, norm_g=(g_mem3, i), name="mem_v")
        mem_k.append(k.reshape(n_seq, mem_tokens, heads, head_dim))
        mem_v.append(v.reshape(n_seq, mem_tokens, heads, head_dim))
        q = _mm(xp, [(w_q, i, 0)], d, _ep_plain, BF16, norm_g=(g_xattn3, i), name="q", **wide)
        o = _attn_prompt(q, k.reshape(n_seq, mem_tokens, d), v.reshape(n_seq, mem_tokens, d),
                         n_seq, seq, heads)
        xp = _mm(o, [(w_o, i, 0)], d, _ep_residual, F32, full_extra=[xp], name="attn_out", **wide)

        q = _mm(xs, [(w_q, i, 0)], d, _ep_plain, F32, norm_g=(g_xattn3, i), name="q_s")
        o = _attn_sample(_slab_view(q.reshape(bsz, heads, head_dim), heads), cache_k_view,
                         cache_v_view, i, heads)
        xs = _mm(_slab_unview(o, heads).reshape(bsz, d), [(w_o, i, 0)], d, _ep_residual, F32,
                 full_extra=[xs], name="attn_out_s")

        xs, wu_bf, wd_bf = _mlp_cast(xs, (g_mlp3, i), w_up, w_down, i)
        xp = _mlp(xp, (g_mlp3, i), wu_bf, wd_bf)

    g_final3 = g_final.reshape(1, 1, d)
    y_prompt = _rmsnorm(xp, (g_final3, 0), F32).reshape(n_seq, seq, d)
    y_sample = _rmsnorm(xs, (g_final3, 0), F32).reshape(bsz, 1, d)
    return (y_prompt, y_sample,
            jnp.stack(mem_k), jnp.stack(mem_v),
            jnp.stack(lru_conv_p), jnp.stack(lru_h_p), jnp.stack(pool_p),
            jnp.stack(s5_re_p), jnp.stack(s5_im_p),
            jnp.stack(lru_conv_s), jnp.stack(lru_h_s), jnp.stack(pool_s),
            jnp.stack(s5_re_s), jnp.stack(s5_im_s))
```

```python
import functools
import math

import jax
import jax.numpy as jnp
from jax import lax
from jax.experimental import pallas as pl
from jax.experimental.pallas import tpu as pltpu

F32 = jnp.float32
BF16 = jnp.bfloat16

SUBLANES = 8
LANES = 128
VMEM_LIMIT_BYTES = 56 * 1024 * 1024

RMS_EPS = 1e-6
LRU_C = 8.0
POOL_WINDOWS = (2, 4, 8, 16)
POOL_BUF = max(POOL_WINDOWS) - 1
S5_GROUP_DIM = 16
S5_CHUNK = 8
S5_GROUPS_PER_BLOCK = LANES // S5_GROUP_DIM


def _params(n_axes):
    return pltpu.CompilerParams(dimension_semantics=("arbitrary",) * n_axes,
                                vmem_limit_bytes=VMEM_LIMIT_BYTES)


def _rms(x, g):
    xf = x.astype(F32)
    inv = lax.rsqrt(jnp.mean(xf * xf, axis=-1, keepdims=True) + RMS_EPS)
    return xf * inv * g.astype(F32)


def _rmsnorm_kernel(x_ref, g_ref, o_ref):
    o_ref[...] = _rms(x_ref[...], g_ref[0]).astype(o_ref.dtype)


def _rmsnorm(x, gain, out_dtype, tm=512):
    rows, d = x.shape
    tm = min(tm, rows)
    g, layer = gain
    return pl.pallas_call(
        _rmsnorm_kernel,
        grid=(rows // tm,),
        in_specs=[pl.BlockSpec((tm, d), lambda i: (i, 0)),
                  pl.BlockSpec((1, 1, d), lambda i: (layer, 0, 0))],
        out_specs=pl.BlockSpec((tm, d), lambda i: (i, 0)),
        out_shape=jax.ShapeDtypeStruct((rows, d), out_dtype),
        compiler_params=_params(1),
        name="rmsnorm",
    )(x, g)


def _mm_kernel(*refs, n_w, n_row, n_full, has_norm, n_lhs, lhs_fn, epilogue):
    x_ref = refs[0]
    refs = refs[1:]
    g_ref = refs[0] if has_norm else None
    refs = refs[1:] if has_norm else refs
    lhs_refs = refs[:n_lhs]
    refs = refs[n_lhs:]
    w_refs = refs[:n_w]
    row_refs = refs[n_w:n_w + n_row]
    full_refs = refs[n_w + n_row:n_w + n_row + n_full]
    o_ref = refs[n_w + n_row + n_full]
    wbf_refs = refs[1 + n_w + n_row + n_full:]

    @pl.when(pl.program_id(1) == 0)
    def _():
        for w_ref, wbf_ref in zip(w_refs, wbf_refs):
            wbf_ref[...] = w_ref[0].astype(BF16)

    if has_norm:
        xb = _rms(x_ref[...], g_ref[0]).astype(BF16)
    elif lhs_fn is not None:
        xb = lhs_fn(x_ref[...], *[r[...] for r in lhs_refs]).astype(BF16)
    else:
        xb = x_ref[...].astype(BF16)
    accs = [jnp.dot(xb, wbf_ref[...], preferred_element_type=F32) for wbf_ref in wbf_refs]
    out = epilogue(accs, [r[0] for r in row_refs], [f[...] for f in full_refs])
    o_ref[...] = out.astype(o_ref.dtype)


def _mm(x, ws, n_cols, epilogue, out_dtype, *, norm_g=None, lhs_fn=None, lhs_extra=(),
        rows_extra=(), full_extra=(), tm=1024, tn=512, single_w=False, name="mm"):
    rows, k = x.shape
    tm = min(tm, rows)
    tn = min(tn, n_cols)
    in_specs = [pl.BlockSpec((tm, k), lambda j, i: (i, 0))]
    args = [x]
    if norm_g is not None:
        g, g_layer = norm_g
        in_specs.append(pl.BlockSpec((1, 1, k), lambda j, i: (g_layer, 0, 0)))
        args.append(g)
    for e in lhs_extra:
        if e.shape[0] == 1:
            in_specs.append(pl.BlockSpec((1, k), lambda j, i: (0, 0)))
        else:
            in_specs.append(pl.BlockSpec((tm, k), lambda j, i: (i, 0)))
        args.append(e)
    w_mode = dict(pipeline_mode=pl.Buffered(1)) if single_w else {}
    for w, layer, off in ws:
        in_specs.append(pl.BlockSpec((1, k, tn), lambda j, i, l=layer, o=off // tn: (l, 0, o + j),
                                     **w_mode))
        args.append(w)
    for v, layer, off in rows_extra:
        in_specs.append(pl.BlockSpec((1, 1, tn), lambda j, i, l=layer, o=off // tn: (l, 0, o + j)))
        args.append(v)
    for f in full_extra:
        in_specs.append(pl.BlockSpec((tm, tn), lambda j, i: (i, j)))
        args.append(f)
    kern = functools.partial(_mm_kernel, n_w=len(ws), n_row=len(rows_extra),
                             n_full=len(full_extra), has_norm=norm_g is not None,
                             n_lhs=len(lhs_extra), lhs_fn=lhs_fn, epilogue=epilogue)
    return pl.pallas_call(
        kern,
        grid=(n_cols // tn, rows // tm),
        in_specs=in_specs,
        out_specs=pl.BlockSpec((tm, tn), lambda j, i: (i, j)),
        out_shape=jax.ShapeDtypeStruct((rows, n_cols), out_dtype),
        scratch_shapes=[pltpu.VMEM((k, tn), BF16) for _ in ws],
        compiler_params=_params(2),
        name=name,
    )(*args)


def _ep_plain(accs, rows, fulls):
    return accs[0]


def _ep_residual(accs, rows, fulls):
    return fulls[0] + accs[0]


def _ep_glu_residual(accs, rows, fulls):
    a = accs[0] + rows[0]
    b = accs[1] + rows[1]
    return fulls[0] + a * jax.nn.sigmoid(b)


def _mlp_kernel(x_ref, g_ref, wu_ref, wd_ref, o_ref, xn_ref):
    @pl.when(pl.program_id(1) == 0)
    def _():
        xv = x_ref[...]
        xn_ref[...] = _rms(xv, g_ref[0]).astype(BF16)
        o_ref[...] = xv

    h = jnp.dot(xn_ref[...], wu_ref[...], preferred_element_type=F32)
    h = jnp.maximum(h, 0.0)
    o_ref[...] += jnp.dot((h * h).astype(BF16), wd_ref[...], preferred_element_type=F32)


def _mlp(x, gain, wu_bf, wd_bf, tm=1024, tf=512):
    rows, d = x.shape
    dff = wu_bf.shape[1]
    tm = min(tm, rows)
    g, layer = gain
    return pl.pallas_call(
        _mlp_kernel,
        grid=(rows // tm, dff // tf),
        in_specs=[pl.BlockSpec((tm, d), lambda i, f: (i, 0)),
                  pl.BlockSpec((1, 1, d), lambda i, f: (layer, 0, 0)),
                  pl.BlockSpec((d, tf), lambda i, f: (0, f)),
                  pl.BlockSpec((tf, d), lambda i, f: (f, 0))],
        out_specs=pl.BlockSpec((tm, d), lambda i, f: (i, 0)),
        out_shape=jax.ShapeDtypeStruct((rows, d), F32),
        scratch_shapes=[pltpu.VMEM((tm, d), BF16)],
        compiler_params=_params(2),
        name="mlp",
    )(x, g, wu_bf, wd_bf)


def _mlp_cast_kernel(x_ref, g_ref, wu_ref, wd_ref, o_ref, wub_ref, wdb_ref, xn_ref):
    @pl.when(pl.program_id(0) == 0)
    def _():
        xv = x_ref[...]
        xn_ref[...] = _rms(xv, g_ref[0]).astype(BF16)
        o_ref[...] = xv

    wu = wu_ref[0].astype(BF16)
    wd = wd_ref[0].astype(BF16)
    wub_ref[...] = wu
    wdb_ref[...] = wd
    h = jnp.dot(xn_ref[...], wu, preferred_element_type=F32)
    h = jnp.maximum(h, 0.0)
    o_ref[...] += jnp.dot((h * h).astype(BF16), wd, preferred_element_type=F32)


def _mlp_cast(x, gain, w_up, w_down, layer, tf=512):
    rows, d = x.shape
    dff = w_up.shape[2]
    g, g_layer = gain
    return pl.pallas_call(
        _mlp_cast_kernel,
        grid=(dff // tf,),
        in_specs=[pl.BlockSpec((rows, d), lambda f: (0, 0)),
                  pl.BlockSpec((1, 1, d), lambda f: (g_layer, 0, 0)),
                  pl.BlockSpec((1, d, tf), lambda f: (layer, 0, f)),
                  pl.BlockSpec((1, tf, d), lambda f: (layer, f, 0))],
        out_specs=[pl.BlockSpec((rows, d), lambda f: (0, 0)),
                   pl.BlockSpec((d, tf), lambda f: (0, f)),
                   pl.BlockSpec((tf, d), lambda f: (f, 0))],
        out_shape=[jax.ShapeDtypeStruct((rows, d), F32),
                   jax.ShapeDtypeStruct((d, dff), BF16),
                   jax.ShapeDtypeStruct((dff, d), BF16)],
        scratch_shapes=[pltpu.VMEM((rows, d), BF16)],
        compiler_params=_params(1),
        name="mlp_cast",
    )(x, g, w_up, w_down)


def _attn_prompt_kernel(q_ref, k_ref, v_ref, o_ref, *, scale):
    q = q_ref[...]
    k = k_ref[0].astype(BF16)
    s = lax.dot_general(q, k, (((1,), (1,)), ((), ())), preferred_element_type=F32) * scale
    m = jnp.max(s, axis=-1, keepdims=True)
    e = jnp.exp(s - m)
    p = e / jnp.sum(e, axis=-1, keepdims=True)
    o = jnp.dot(p.astype(BF16), v_ref[0].astype(BF16), preferred_element_type=F32)
    o_ref[...] = o.astype(o_ref.dtype)


def _attn_prompt(q, k, v, n_seq, seq, heads, tq=1024):
    rows, d = q.shape
    m = k.shape[1]
    hd = d // heads
    tq = min(tq, seq)
    tps = seq // tq
    kern = functools.partial(_attn_prompt_kernel, scale=hd ** -0.5)
    return pl.pallas_call(
        kern,
        grid=(n_seq, tps, heads),
        in_specs=[pl.BlockSpec((tq, hd), lambda b, i, h: (b * tps + i, h)),
                  pl.BlockSpec((1, m, hd), lambda b, i, h: (b, 0, h)),
                  pl.BlockSpec((1, m, hd), lambda b, i, h: (b, 0, h))],
        out_specs=pl.BlockSpec((tq, hd), lambda b, i, h: (b * tps + i, h)),
        out_shape=jax.ShapeDtypeStruct((rows, d), BF16),
        compiler_params=_params(3),
        name="attn_prompt",
    )(q, k, v)


def _slab_view(x, heads):
    lead = x.shape[:-2]
    hd = x.shape[-1]
    n = len(lead)
    x = x.reshape(lead + (heads, hd // LANES, LANES))
    x = x.transpose(tuple(range(n)) + (n + 1, n, n + 2))
    return x.reshape(lead + (heads * hd // (SUBLANES * LANES), SUBLANES, LANES))


def _slab_unview(x, heads):
    lead = x.shape[:-3]
    n = len(lead)
    tiles = x.shape[-3] * SUBLANES // heads
    x = x.reshape(lead + (tiles, heads, LANES))
    x = x.transpose(tuple(range(n)) + (n + 1, n, n + 2))
    return x.reshape(lead + (heads, tiles * LANES))


def _attn_sample_kernel(q_ref, k_ref, v_ref, o_ref, *, heads, scale, bt):
    for b in range(bt):
        t = jnp.sum(k_ref[0, b] * q_ref[b][None], axis=1)
        t = t + pltpu.roll(t, heads, 1)
        s = jnp.sum(t, axis=-1, keepdims=True) * scale
        mx = jnp.max(s, axis=0, keepdims=True)
        e = jnp.exp(s - mx)
        p = e / jnp.sum(e, axis=0, keepdims=True)
        o_ref[b] = jnp.sum(p[:, None] * v_ref[0, b], axis=0)


def _attn_sample(q, cache_k, cache_v, layer, heads, bt=2):
    bsz, pairs = q.shape[:2]
    m = cache_k.shape[2]
    hd = pairs * SUBLANES * LANES // heads
    kern = functools.partial(_attn_sample_kernel, heads=heads, scale=hd ** -0.5, bt=bt)
    kv_spec = pl.BlockSpec((1, bt, m, pairs, SUBLANES, LANES), lambda i: (layer, i, 0, 0, 0, 0))
    q_spec = pl.BlockSpec((bt, pairs, SUBLANES, LANES), lambda i: (i, 0, 0, 0))
    return pl.pallas_call(
        kern,
        grid=(bsz // bt,),
        in_specs=[q_spec, kv_spec, kv_spec],
        out_specs=q_spec,
        out_shape=jax.ShapeDtypeStruct(q.shape, F32),
        compiler_params=_params(1),
        name="attn_sample",
    )(q, cache_k, cache_v)


def _lru_gates(xc, wa, ba, wi, bi, lam):
    xcb = xc.astype(BF16)
    r = jax.nn.sigmoid(jnp.dot(xcb, wa.astype(BF16), preferred_element_type=F32) + ba)
    ig = jax.nn.sigmoid(jnp.dot(xcb, wi.astype(BF16), preferred_element_type=F32) + bi)
    log_a = -LRU_C * r * jax.nn.softplus(-lam)
    a = jnp.exp(log_a)
    u = jnp.sqrt(-jnp.tanh(log_a) * (a * a + 1.0)) * (ig * xc)
    return a, u


def _lru_prompt_kernel(xb_ref, gt_ref, cw_ref, cb_ref, wa_ref, ba_ref, wi_ref, bi_ref, lam_ref,
                       y_ref, cst_ref, hl_ref, ext_ref, a_ref, u_ref, hs_ref, hc_ref,
                       *, tiles_per_seq, taps):
    tm, cb = xb_ref.shape
    halo = SUBLANES

    @pl.when(pl.program_id(1) % tiles_per_seq == 0)
    def _():
        ext_ref[0:halo, :] = jnp.zeros((halo, cb), F32)
        hc_ref[...] = jnp.zeros((1, cb), F32)

    ext_ref[halo:halo + tm, :] = xb_ref[...]
    cw = cw_ref[...]
    xc = cb_ref[...]
    for k in range(taps):
        start = halo - (taps - 1) + k
        xc = xc + cw[k:k + 1, :] * ext_ref[start:start + tm, :]
    cst_ref[0] = ext_ref[halo + tm - (taps - 1):halo + tm, :]
    ext_ref[0:halo, :] = ext_ref[tm:tm + halo, :]

    a, u = _lru_gates(xc, wa_ref[0], ba_ref[...], wi_ref[0], bi_ref[...], lam_ref[...])

    groups = tm // SUBLANES
    a = a.reshape(groups, SUBLANES, cb)
    u = u.reshape(groups, SUBLANES, cb)
    row = lax.broadcasted_iota(jnp.int32, (groups, SUBLANES, cb), 1)
    s = 1
    while s < SUBLANES:
        keep = row >= s
        u = u + a * jnp.where(keep, pltpu.roll(u, s, 1), 0.0)
        a = a * jnp.where(keep, pltpu.roll(a, s, 1), 1.0)
        s *= 2
    a_ref[...] = a.reshape(tm, cb)
    u_ref[...] = u.reshape(tm, cb)

    def link(g, h):
        off = pl.multiple_of(g * SUBLANES, SUBLANES)
        ag = a_ref[pl.ds(off, SUBLANES), :]
        ug = u_ref[pl.ds(off, SUBLANES), :]
        hs_ref[pl.ds(off, SUBLANES), :] = ag * h + ug
        return ag[SUBLANES - 1:SUBLANES, :] * h + ug[SUBLANES - 1:SUBLANES, :]

    h_last = lax.fori_loop(0, tm // SUBLANES, link, hc_ref[...], unroll=4)
    hc_ref[...] = h_last
    hl_ref[0] = h_last
    y_ref[...] = (hs_ref[...] * jax.nn.gelu(gt_ref[...])).astype(y_ref.dtype)


def _lru_prompt(proj, conv_w, conv_b, w_a, b_a, w_i, b_i, lam, n_seq, seq, tm=1024):
    rows, r2 = proj.shape
    r = r2 // 2
    nb, cb = w_a.shape[0], w_a.shape[1]
    taps = conv_w.shape[0]
    tm = min(tm, seq)
    tps = seq // tm
    kern = functools.partial(_lru_prompt_kernel, tiles_per_seq=tps, taps=taps)
    vec = lambda c, i: (0, c)
    y, cst, hl = pl.pallas_call(
        kern,
        grid=(nb, rows // tm),
        in_specs=[pl.BlockSpec((tm, cb), lambda c, i: (i, c)),
                  pl.BlockSpec((tm, cb), lambda c, i: (i, nb + c)),
                  pl.BlockSpec((taps, cb), vec),
                  pl.BlockSpec((1, cb), vec),
                  pl.BlockSpec((1, cb, cb), lambda c, i: (c, 0, 0)),
                  pl.BlockSpec((1, cb), vec),
                  pl.BlockSpec((1, cb, cb), lambda c, i: (c, 0, 0)),
                  pl.BlockSpec((1, cb), vec),
                  pl.BlockSpec((1, cb), vec)],
        out_specs=[pl.BlockSpec((tm, cb), lambda c, i: (i, c)),
                   pl.BlockSpec((1, taps - 1, cb), lambda c, i: (i // tps, 0, c)),
                   pl.BlockSpec((1, 1, cb), lambda c, i: (i // tps, 0, c))],
        out_shape=[jax.ShapeDtypeStruct((rows, r), BF16),
                   jax.ShapeDtypeStruct((n_seq, taps - 1, r), F32),
                   jax.ShapeDtypeStruct((n_seq, 1, r), F32)],
        scratch_shapes=[pltpu.VMEM((tm + SUBLANES, cb), F32),
                        pltpu.VMEM((tm, cb), F32),
                        pltpu.VMEM((tm, cb), F32),
                        pltpu.VMEM((tm, cb), F32),
                        pltpu.VMEM((1, cb), F32)],
        compiler_params=_params(2),
        name="lru_prompt",
    )(proj, proj, conv_w, conv_b.reshape(1, r), w_a, b_a.reshape(1, r), w_i, b_i.reshape(1, r),
      lam.reshape(1, r))
    return y, cst, hl.reshape(n_seq, r)


def _lru_sample_kernel(xb_ref, gt_ref, c0_ref, c1_ref, c2_ref, h0_ref, cw_ref, cb_ref,
                       wa_ref, ba_ref, wi_ref, bi_ref, lam_ref, y_ref, h_ref):
    cw = cw_ref[...]
    xc = cb_ref[...]
    for k, c_ref in enumerate((c0_ref, c1_ref, c2_ref, xb_ref)):
        xc = xc + cw[k:k + 1, :] * c_ref[...]
    a, u = _lru_gates(xc, wa_ref[0], ba_ref[...], wi_ref[0], bi_ref[...], lam_ref[...])
    h = a * h0_ref[...] + u
    h_ref[...] = h
    y_ref[...] = (h * jax.nn.gelu(gt_ref[...])).astype(y_ref.dtype)


def _lru_sample(proj, conv_rows, h0, conv_w, conv_b, w_a, b_a, w_i, b_i, lam):
    bsz, r2 = proj.shape
    r = r2 // 2
    nb, cb = w_a.shape[0], w_a.shape[1]
    taps = conv_w.shape[0]
    blk = lambda c: (0, c)
    return pl.pallas_call(
        _lru_sample_kernel,
        grid=(nb,),
        in_specs=[pl.BlockSpec((bsz, cb), blk),
                  pl.BlockSpec((bsz, cb), lambda c: (0, nb + c)),
                  pl.BlockSpec((bsz, cb), blk), pl.BlockSpec((bsz, cb), blk),
                  pl.BlockSpec((bsz, cb), blk), pl.BlockSpec((bsz, cb), blk),
                  pl.BlockSpec((taps, cb), blk), pl.BlockSpec((1, cb), blk),
                  pl.BlockSpec((1, cb, cb), lambda c: (c, 0, 0)), pl.BlockSpec((1, cb), blk),
                  pl.BlockSpec((1, cb, cb), lambda c: (c, 0, 0)), pl.BlockSpec((1, cb), blk),
                  pl.BlockSpec((1, cb), blk)],
        out_specs=[pl.BlockSpec((bsz, cb), blk), pl.BlockSpec((bsz, cb), blk)],
        out_shape=[jax.ShapeDtypeStruct((bsz, r), BF16), jax.ShapeDtypeStruct((bsz, r), F32)],
        compiler_params=_params(1),
        name="lru_sample",
    )(proj, proj, conv_rows[0], conv_rows[1], conv_rows[2], h0, conv_w, conv_b.reshape(1, r),
      w_a, b_a.reshape(1, r), w_i, b_i.reshape(1, r), lam.reshape(1, r))


def _pool_prompt_kernel(x_ref, g_ref, w_ref, b_ref, sc_ref, o_ref, st_ref, ext_ref, wbf_ref,
                        *, tiles_per_seq, windows):
    tm, d = x_ref.shape
    gd = d // len(windows)
    halo = 2 * SUBLANES
    i = pl.program_id(0)

    @pl.when(i == 0)
    def _():
        wbf_ref[...] = w_ref[...].astype(BF16)

    @pl.when(i % tiles_per_seq == 0)
    def _():
        ext_ref[0:halo, :] = jnp.zeros((halo, d), F32)

    ext_ref[halo:halo + tm, :] = _rms(x_ref[...], g_ref[0])
    pos = (i % tiles_per_seq) * tm + lax.broadcasted_iota(jnp.int32, (tm, 1), 0)
    for g, w in enumerate(windows):
        sl = slice(g * gd, (g + 1) * gd)
        s = ext_ref[:, sl]
        shift = 1
        while shift < w:
            s = s + pltpu.roll(s, shift, 0)
            shift *= 2
        cnt = jnp.minimum(pos + 1, w).astype(F32)
        pooled = s[halo:, :] / cnt
        diff = (pooled - ext_ref[halo:halo + tm, sl]).astype(BF16)
        mixed = jnp.dot(diff, wbf_ref[g], preferred_element_type=F32) + b_ref[:, sl]
        o_ref[:, sl] = x_ref[:, sl] + mixed * sc_ref[:, sl]
    st_ref[0] = ext_ref[halo + tm - POOL_BUF:halo + tm, :]
    ext_ref[0:halo, :] = ext_ref[tm:tm + halo, :]


def _pool_prompt(x, gain, w, b, scale, n_seq, seq, tm=512):
    rows, d = x.shape
    tm = min(tm, seq)
    tps = seq // tm
    ng, gd = w.shape[0], w.shape[1]
    g, layer = gain
    kern = functools.partial(_pool_prompt_kernel, tiles_per_seq=tps, windows=POOL_WINDOWS)
    return pl.pallas_call(
        kern,
        grid=(rows // tm,),
        in_specs=[pl.BlockSpec((tm, d), lambda i: (i, 0)),
                  pl.BlockSpec((1, 1, d), lambda i: (layer, 0, 0)),
                  pl.BlockSpec((ng, gd, gd), lambda i: (0, 0, 0)),
                  pl.BlockSpec((1, d), lambda i: (0, 0)),
                  pl.BlockSpec((1, d), lambda i: (0, 0))],
        out_specs=[pl.BlockSpec((tm, d), lambda i: (i, 0)),
                   pl.BlockSpec((1, POOL_BUF, d), lambda i: (i // tps, 0, 0))],
        out_shape=[jax.ShapeDtypeStruct((rows, d), F32),
                   jax.ShapeDtypeStruct((n_seq, POOL_BUF, d), F32)],
        scratch_shapes=[pltpu.VMEM((tm + 2 * SUBLANES, d), F32),
                        pltpu.VMEM((ng, gd, gd), BF16)],
        compiler_params=_params(1),
        name="pool_prompt",
    )(x, g, w, b.reshape(1, d), scale.reshape(1, d))


def _pool_sample_kernel(xn_ref, x_ref, st_ref, w_ref, b_ref, sc_ref, o_ref, *, window, pos0):
    xn = xn_ref[...]
    s = xn
    for r in range(POOL_BUF - (window - 1), POOL_BUF):
        s = s + st_ref[r]
    pooled = s / float(min(pos0 + 1, window))
    diff = (pooled - xn).astype(BF16)
    mixed = jnp.dot(diff, w_ref[0].astype(BF16), preferred_element_type=F32) + b_ref[...]
    o_ref[...] = x_ref[...] + mixed * sc_ref[...]


def _pool_sample(xn, x, state_t, w, b, scale, pos0):
    bsz, d = xn.shape
    ng, gd = w.shape[0], w.shape[1]
    outs = []
    for g, window in enumerate(POOL_WINDOWS):
        kern = functools.partial(_pool_sample_kernel, window=window, pos0=pos0)
        blk = lambda i, g=g: (0, g)
        outs.append(pl.pallas_call(
            kern,
            grid=(1,),
            in_specs=[pl.BlockSpec((bsz, gd), blk), pl.BlockSpec((bsz, gd), blk),
                      pl.BlockSpec((POOL_BUF, bsz, gd), lambda i, g=g: (0, 0, g)),
                      pl.BlockSpec((1, gd, gd), lambda i, g=g: (g, 0, 0)),
                      pl.BlockSpec((1, gd), blk), pl.BlockSpec((1, gd), blk)],
            out_specs=pl.BlockSpec((bsz, gd), lambda i: (0, 0)),
            out_shape=jax.ShapeDtypeStruct((bsz, gd), F32),
            compiler_params=_params(1),
            name="pool_sample",
        )(xn, x, state_t, w, b.reshape(1, d), scale.reshape(1, d)))
    return jnp.concatenate(outs, axis=-1)


def _s5_matrices(a_re, a_im, b_re, b_im, c_re, c_im, log_dt, chunk):
    g_all, p = a_re.shape
    c = b_re.shape[-1]
    gpb = S5_GROUPS_PER_BLOCK
    nb = g_all // gpb
    dt = jnp.exp(log_dt.astype(F32))[:, None]
    mag = jnp.exp(a_re * dt)
    lr, li = mag * jnp.cos(a_im * dt), mag * jnp.sin(a_im * dt)
    den = a_re * a_re + a_im * a_im
    q_re = ((lr - 1.0) * a_re + li * a_im) / den
    q_im = (li * a_re - (lr - 1.0) * a_im) / den
    bb_re = q_re[..., None] * b_re - q_im[..., None] * b_im
    bb_im = q_re[..., None] * b_im + q_im[..., None] * b_re

    def cmul(xr, xi, yr, yi):
        return xr * yr - xi * yi, xr * yi + xi * yr

    pw = [(jnp.ones_like(lr), jnp.zeros_like(lr))]
    for _ in range(chunk):
        pw.append(cmul(pw[-1][0], pw[-1][1], lr, li))
    hi = lax.Precision.HIGHEST
    lane_g = jnp.arange(LANES) // c
    state_g = jnp.arange(gpb * p) // p
    mask_cc = (lane_g[:, None] == lane_g[None, :]).astype(F32)
    mask_cp = (lane_g[:, None] == state_g[None, :]).astype(F32)
    spread = jnp.tile(jnp.eye(p, dtype=F32), (1, gpb))

    def lanes_by_state(x):
        return x.transpose(0, 1, 3, 2).reshape(x.shape[0], nb, LANES, p)

    def state_by_lanes(x):
        return x.transpose(0, 3, 1, 2).reshape(x.shape[0], p, nb, LANES).transpose(0, 2, 1, 3)

    pr_all = jnp.stack([x[0] for x in pw])
    pi_all = jnp.stack([x[1] for x in pw])
    wr, wi = cmul(c_re[None], c_im[None], pr_all[:, :, None, :], pi_all[:, :, None, :])
    wr, wi = state_by_lanes(wr), state_by_lanes(wi)

    bb_rows = jnp.concatenate([lanes_by_state(bb_re[None])[0], -lanes_by_state(bb_im[None])[0]],
                              axis=-1)
    w_cols = jnp.concatenate([wr[:chunk], wi[:chunk]], axis=2)
    blocks = mask_cc * jnp.einsum('brk,tbkc->tbrc', bb_rows, w_cols, precision=hi)
    zero = jnp.zeros_like(blocks[0])
    m_in = jnp.stack([jnp.concatenate([blocks[t - s] if t >= s else zero for t in range(chunk)],
                                      axis=-1) for s in range(chunk)], axis=1)
    m_in = m_in.reshape(nb, chunk * LANES, chunk * LANES).astype(BF16)

    pr_rev, pi_rev = pr_all[:chunk][::-1], pi_all[:chunk][::-1]
    er, ei = cmul(pr_rev[..., None], pi_rev[..., None], bb_re[None], bb_im[None])
    halves = [mask_cp * jnp.einsum('tbrp,pq->tbrq', lanes_by_state(e), spread, precision=hi)
              for e in (er, ei)]
    m_state = jnp.concatenate(halves, axis=-1).transpose(1, 0, 2, 3)
    m_state = m_state.reshape(nb, chunk * LANES, 2 * gpb * p).astype(BF16)

    halves = [mask_cp.T * jnp.einsum('qp,tbpc->tbqc', spread.T, w, precision=hi)
              for w in (wr[1:], -wi[1:])]
    m_out = jnp.concatenate(halves, axis=2).transpose(1, 2, 0, 3)
    m_out = m_out.reshape(nb, 2 * gpb * p, chunk * LANES).astype(BF16)

    qr, qi = pw[chunk]
    rows = [(jnp.ones_like(qr), jnp.zeros_like(qr))]
    for _ in range(SUBLANES):
        rows.append(cmul(rows[-1][0], rows[-1][1], qr, qi))
    table = jnp.stack([_s5_state_layout(r_, i_) for r_, i_ in rows])
    table = jnp.concatenate([table, jnp.zeros((2 * SUBLANES - table.shape[0], table.shape[1]), F32)])
    return m_in, m_state, m_out, table


def _s5_state_layout(re, im):
    lead = re.shape[:-2]
    g_all, p = re.shape[-2:]
    nb = g_all // S5_GROUPS_PER_BLOCK
    st = jnp.stack([re.reshape(lead + (nb, S5_GROUPS_PER_BLOCK * p)),
                    im.reshape(lead + (nb, S5_GROUPS_PER_BLOCK * p))], axis=-2)
    return st.reshape(lead + (2 * g_all * p,))


def _s5_state_unlayout(flat, g_all, p):
    lead = flat.shape[:-1]
    nb = g_all // S5_GROUPS_PER_BLOCK
    st = flat.reshape(lead + (nb, 2, S5_GROUPS_PER_BLOCK * p))
    return st[..., 0, :].reshape(lead + (g_all, p)), st[..., 1, :].reshape(lead + (g_all, p))


def _s5_chunk_rows(t, tr, chunk):
    return pl.ds(t, tr, stride=chunk) if chunk > 1 else pl.ds(0, tr)


def _s5_lhs(u_ref, chunk):
    tr = u_ref.shape[0] // chunk
    parts = [u_ref[_s5_chunk_rows(t, tr, chunk), :] for t in range(chunk)]
    return jnp.concatenate(parts, axis=1).astype(BF16)


def _s5_state_in_kernel(u_ref, m_ref, e_ref, *, chunk):
    e_ref[...] = jnp.dot(_s5_lhs(u_ref, chunk), m_ref[0], preferred_element_type=F32)


def _s5_state_in(xn, m_state, chunk, tr=1024):
    rows, d = xn.shape
    r = rows // chunk
    nb, _, sw = m_state.shape
    tr = min(tr, r)
    kern = functools.partial(_s5_state_in_kernel, chunk=chunk)
    return pl.pallas_call(
        kern,
        grid=(nb, r // tr),
        in_specs=[pl.BlockSpec((tr * chunk, LANES), lambda b, i: (i, b)),
                  pl.BlockSpec((1, chunk * LANES, sw), lambda b, i: (b, 0, 0))],
        out_specs=pl.BlockSpec((tr, sw), lambda b, i: (i, b)),
        out_shape=jax.ShapeDtypeStruct((r, nb * sw), F32),
        compiler_params=_params(2),
        name="s5_state_in",
    )(xn, m_state)


def _s5_scan_kernel(e_ref, tab_ref, h_ref, fin_ref):
    rps, sw = e_ref.shape
    half = sw // 2
    groups = rps // SUBLANES
    er = e_ref[:, :half].reshape(groups, SUBLANES, half)
    ei = e_ref[:, half:].reshape(groups, SUBLANES, half)
    row = lax.broadcasted_iota(jnp.int32, (groups, SUBLANES, half), 1)
    s = 1
    while s < SUBLANES:
        keep = row >= s
        pr, pi = tab_ref[s:s + 1, :half], tab_ref[s:s + 1, half:]
        sr = jnp.where(keep, pltpu.roll(er, s, 1), 0.0)
        si = jnp.where(keep, pltpu.roll(ei, s, 1), 0.0)
        er, ei = er + pr * sr - pi * si, ei + pr * si + pi * sr
        s *= 2
    xr = jnp.where(row >= 1, pltpu.roll(er, 1, 1), 0.0)
    xi = jnp.where(row >= 1, pltpu.roll(ei, 1, 1), 0.0)
    ar, ai = tab_ref[0:SUBLANES, :half], tab_ref[0:SUBLANES, half:]
    l8r, l8i = tab_ref[SUBLANES:SUBLANES + 1, :half], tab_ref[SUBLANES:SUBLANES + 1, half:]
    cr = jnp.zeros((1, half), F32)
    ci = jnp.zeros((1, half), F32)
    for g in range(groups):
        lo, hi = g * SUBLANES, (g + 1) * SUBLANES
        h_ref[lo:hi, :half] = ar * cr - ai * ci + xr[g]
        h_ref[lo:hi, half:] = ar * ci + ai * cr + xi[g]
        cr, ci = (l8r * cr - l8i * ci + er[g, SUBLANES - 1:SUBLANES],
                  l8r * ci + l8i * cr + ei[g, SUBLANES - 1:SUBLANES])
    fin_ref[0, :, :half] = cr
    fin_ref[0, :, half:] = ci


def _s5_scan(e, table, n_seq, sw):
    r, width = e.shape
    rps = r // n_seq
    return pl.pallas_call(
        _s5_scan_kernel,
        grid=(width // sw, n_seq),
        in_specs=[pl.BlockSpec((rps, sw), lambda c, b: (b, c)),
                  pl.BlockSpec((2 * SUBLANES, sw), lambda c, b: (0, c))],
        out_specs=[pl.BlockSpec((rps, sw), lambda c, b: (b, c)),
                   pl.BlockSpec((1, 1, sw), lambda c, b: (b, 0, c))],
        out_shape=[jax.ShapeDtypeStruct((r, width), F32),
                   jax.ShapeDtypeStruct((n_seq, 1, width), F32)],
        compiler_params=_params(2),
        name="s5_scan",
    )(e, table)


def _s5_step_kernel(e_ref, h0_ref, tab_ref, o_ref):
    half = e_ref.shape[1] // 2
    lr, li = tab_ref[1:2, :half], tab_ref[1:2, half:]
    hr, hi = h0_ref[:, :half], h0_ref[:, half:]
    o_ref[:, :half] = lr * hr - li * hi + e_ref[:, :half]
    o_ref[:, half:] = lr * hi + li * hr + e_ref[:, half:]


def _s5_step(e, h0, table, sw):
    bsz, width = e.shape
    return pl.pallas_call(
        _s5_step_kernel,
        grid=(width // sw,),
        in_specs=[pl.BlockSpec((bsz, sw), lambda c: (0, c)),
                  pl.BlockSpec((bsz, sw), lambda c: (0, c)),
                  pl.BlockSpec((2 * SUBLANES, sw), lambda c: (0, c))],
        out_specs=pl.BlockSpec((bsz, sw), lambda c: (0, c)),
        out_shape=jax.ShapeDtypeStruct((bsz, width), F32),
        compiler_params=_params(1),
        name="s5_step",
    )(e, h0, table)


def _s5_out_kernel(u_ref, h_ref, mi_ref, mo_ref, y_ref, *, chunk):
    tr = h_ref.shape[0]
    y = jnp.dot(_s5_lhs(u_ref, chunk), mi_ref[0], preferred_element_type=F32)
    y = y + jnp.dot(h_ref[...].astype(BF16), mo_ref[0], preferred_element_type=F32)
    for t in range(chunk):
        y_ref[_s5_chunk_rows(t, tr, chunk), :] = y[:, t * LANES:(t + 1) * LANES]


def _s5_out(xn, h, m_in, m_out, chunk, tr=1024):
    rows, d = xn.shape
    r = rows // chunk
    nb, sw, _ = m_out.shape
    tr = min(tr, r)
    kern = functools.partial(_s5_out_kernel, chunk=chunk)
    return pl.pallas_call(
        kern,
        grid=(nb, r // tr),
        in_specs=[pl.BlockSpec((tr * chunk, LANES), lambda b, i: (i, b)),
                  pl.BlockSpec((tr, sw), lambda b, i: (i, b)),
                  pl.BlockSpec((1, chunk * LANES, chunk * LANES), lambda b, i: (b, 0, 0)),
                  pl.BlockSpec((1, sw, chunk * LANES), lambda b, i: (b, 0, 0))],
        out_specs=pl.BlockSpec((tr * chunk, LANES), lambda b, i: (i, b)),
        out_shape=jax.ShapeDtypeStruct((rows, d), F32),
        compiler_params=_params(2),
        name="s5_out",
    )(xn, h, m_in, m_out)


def _s5_glu_lhs(y, xn, dvec):
    return jax.nn.gelu(y + dvec * xn)


def _s5_mixer(x, xn, h0_flat, n_seq, prm, w_glu, b_glu, layer, chunk):
    a_re, a_im, b_re, b_im, c_re, c_im, dvec, log_dt = prm
    rows, d = xn.shape
    m_in, m_state, m_out, table = _s5_matrices(a_re, a_im, b_re, b_im, c_re, c_im, log_dt, chunk)
    sw = m_state.shape[-1]
    e = _s5_state_in(xn, m_state, chunk)
    if h0_flat is None:
        h, fin = _s5_scan(e, table, n_seq, sw)
        fin = fin.reshape(n_seq, -1)
    else:
        h = h0_flat
        fin = _s5_step(e, h0_flat, table, sw)
    y = _s5_out(xn, h, m_in, m_out, chunk)
    b_glu3 = b_glu.reshape(b_glu.shape[0], 1, 2 * d)
    tiles = dict(tm=512, tn=1024, single_w=True) if h0_flat is None else {}
    x_new = _mm(y, [(w_glu, layer, 0), (w_glu, layer, d)], d, _ep_glu_residual, F32,
                lhs_fn=_s5_glu_lhs, lhs_extra=[xn, dvec.reshape(1, d)],
                rows_extra=[(b_glu3, layer, 0), (b_glu3, layer, d)],
                full_extra=[x], name="s5_glu", **tiles)
    return x_new, fin


def kernel(x_prompt, x_sample, cache_mem_k, cache_mem_v, state_lru_conv, state_lru_h, state_pool, state_s5_re, state_s5_im, mem_prompt, g_mix, g_xattn, g_mem, g_mlp, g_final, w_q, w_k, w_v, w_o, w_up, w_down, lru_w_in, lru_conv_w, lru_conv_b, lru_w_a, lru_b_a, lru_w_i, lru_b_i, lru_lambda, lru_w_o, pool_w, pool_b, pool_scale, s5_a_re, s5_a_im, s5_b_re, s5_b_im, s5_c_re, s5_c_im, s5_d, s5_log_dt, s5_w_glu, s5_b_glu):
    n_seq, seq, d = x_prompt.shape
    bsz, dec_seq, _ = x_sample.shape
    assert dec_seq == 1, "the sample group advances one token per request"
    depth = g_mix.shape[0]
    mem_tokens = mem_prompt.shape[1]
    heads, head_dim = cache_mem_k.shape[3], cache_mem_k.shape[4]
    g_all, p_state = s5_a_re.shape[1], s5_a_re.shape[2]
    past_len = 16384

    xp = x_prompt.reshape(n_seq * seq, d)
    xs = x_sample.reshape(bsz, d)
    mem = mem_prompt.reshape(n_seq * mem_tokens, d)
    wide = dict(tm=512, tn=d, single_w=True)
    assert SUBLANES == 2 * heads and head_dim % (2 * LANES) == 0, "slab view needs 4 heads"
    cache_k_view = _slab_view(cache_mem_k, heads)
    cache_v_view = _slab_view(cache_mem_v, heads)
    g_mix3, g_xattn3, g_mem3, g_mlp3 = (g.reshape(depth, 1, d)
                                        for g in (g_mix, g_xattn, g_mem, g_mlp))

    mem_k, mem_v = [], []
    lru_conv_p, lru_h_p, lru_conv_s, lru_h_s = [], [], [], []
    pool_p, pool_s = [], []
    s5_re_p, s5_im_p, s5_re_s, s5_im_s = [], [], [], []

    for i in range(depth):
        kind, j = i % 3, i // 3
        if kind == 0:
            prm = (lru_conv_w[j], lru_conv_b[j], lru_w_a[j], lru_b_a[j], lru_w_i[j], lru_b_i[j],
                   lru_lambda[j])
            proj = _mm(xp, [(lru_w_in, j, 0)], 2 * d, _ep_plain, F32, norm_g=(g_mix3, i),
                       name="lru_in", **wide)
            y, cst, hl = _lru_prompt(proj, *prm, n_seq, seq)
            xp = _mm(y, [(lru_w_o, j, 0)], d, _ep_residual, F32, full_extra=[xp], name="lru_out",
                     **wide)
            lru_conv_p.append(cst)
            lru_h_p.append(hl)

            proj = _mm(xs, [(lru_w_in, j, 0)], 2 * d, _ep_plain, F32, norm_g=(g_mix3, i),
                       name="lru_in_s")
            conv_rows = [state_lru_conv[j][:, r, :] for r in range(state_lru_conv.shape[2])]
            y, h_new = _lru_sample(proj, conv_rows, state_lru_h[j], *prm)
            xs = _mm(y, [(lru_w_o, j, 0)], d, _ep_residual, F32, full_extra=[xs], name="lru_out_s")
            lru_conv_s.append(jnp.concatenate(
                [state_lru_conv[j][:, 1:, :], proj[:, None, :d]], axis=1))
            lru_h_s.append(h_new)
        elif kind == 1:
            xp, st = _pool_prompt(xp, (g_mix3, i), pool_w[j], pool_b[j], pool_scale[j], n_seq, seq)
            pool_p.append(st)

            xn = _rmsnorm(xs, (g_mix3, i), F32)
            xs = _pool_sample(xn, xs, jnp.swapaxes(state_pool[j], 0, 1), pool_w[j], pool_b[j],
                              pool_scale[j], past_len)
            pool_s.append(jnp.concatenate([state_pool[j][:, 1:, :], xn[:, None, :]], axis=1))
        else:
            prm = (s5_a_re[j], s5_a_im[j], s5_b_re[j], s5_b_im[j], s5_c_re[j], s5_c_im[j],
                   s5_d[j], s5_log_dt[j])
            xp, fin = _s5_mixer(xp, _rmsnorm(xp, (g_mix3, i), F32), None, n_seq, prm,
                                s5_w_glu, s5_b_glu, j, S5_CHUNK)
            re, im = _s5_state_unlayout(fin, g_all, p_state)
            s5_re_p.append(re)
            s5_im_p.append(im)

            h0 = _s5_state_layout(state_s5_re[j], state_s5_im[j])
            xs, fin = _s5_mixer(xs, _rmsnorm(xs, (g_mix3, i), F32), h0, bsz, prm,
                                s5_w_glu, s5_b_glu, j, 1)
            re, im = _s5_state_unlayout(fin, g_all, p_state)
            s5_re_s.append(re)
            s5_im_s.append(im)

        k = _mm(mem, [(w_k, i, 0)], d, _ep_plain, F32, norm_g=(g_mem3, i), name="mem_k")
        v = _mm(mem, [(w_v, i, 0)], d, _ep_plain, F32, norm_g=(g_mem3, i), name="mem_v")
        mem_k.append(k.reshape(n_seq, mem_tokens, heads, head_dim))
        mem_v.append(v.reshape(n_seq, mem_tokens, heads, head_dim))
        q = _mm(xp, [(w_q, i, 0)], d, _ep_plain, BF16, norm_g=(g_xattn3, i), name="q", **wide)
        o = _attn_prompt(q, k.reshape(n_seq, mem_tokens, d), v.reshape(n_seq, mem_tokens, d),
                         n_seq, seq, heads)
        xp = _mm(o, [(w_o, i, 0)], d, _ep_residual, F32, full_extra=[xp], name="attn_out", **wide)

        q = _mm(xs, [(w_q, i, 0)], d, _ep_plain, F32, norm_g=(g_xattn3, i), name="q_s")
        o = _attn_sample(_slab_view(q.reshape(bsz, heads, head_dim), heads), cache_k_view,
                         cache_v_view, i, heads)
        xs = _mm(_slab_unview(o, heads).reshape(bsz, d), [(w_o, i, 0)], d, _ep_residual, F32,
                 full_extra=[xs], name="attn_out_s")

        xs, wu_bf, wd_bf = _mlp_cast(xs, (g_mlp3, i), w_up, w_down, i)
        xp = _mlp(xp, (g_mlp3, i), wu_bf, wd_bf)

    g_final3 = g_final.reshape(1, 1, d)
    y_prompt = _rmsnorm(xp, (g_final3, 0), F32).reshape(n_seq, seq, d)
    y_sample = _rmsnorm(xs, (g_final3, 0), F32).reshape(bsz, 1, d)
    return (y_prompt, y_sample,
            jnp.stack(mem_k), jnp.stack(mem_v),
            jnp.stack(lru_conv_p), jnp.stack(lru_h_p), jnp.stack(pool_p),
            jnp.stack(s5_re_p), jnp.stack(s5_im_p),
            jnp.stack(lru_conv_s), jnp.stack(lru_h_s), jnp.stack(pool_s),
            jnp.stack(s5_re_s), jnp.stack(s5_im_s))
```

```python
import functools
import math

import jax
import jax.numpy as jnp
from jax import lax
from jax.experimental import pallas as pl
from jax.experimental.pallas import tpu as pltpu

F32 = jnp.float32
BF16 = jnp.bfloat16

SUBLANES = 8
LANES = 128
VMEM_LIMIT_BYTES = 56 * 1024 * 1024

RMS_EPS = 1e-6
LRU_C = 8.0
POOL_WINDOWS = (2, 4, 8, 16)
POOL_BUF = max(POOL_WINDOWS) - 1
S5_GROUP_DIM = 16
S5_CHUNK = 8
S5_GROUPS_PER_BLOCK = LANES // S5_GROUP_DIM


def _params(n_axes):
    return pltpu.CompilerParams(dimension_semantics=("arbitrary",) * n_axes,
                                vmem_limit_bytes=VMEM_LIMIT_BYTES)


def _rms(x, g):
    xf = x.astype(F32)
    inv = lax.rsqrt(jnp.mean(xf * xf, axis=-1, keepdims=True) + RMS_EPS)
    return xf * inv * g.astype(F32)


def _rmsnorm_kernel(x_ref, g_ref, o_ref):
    o_ref[...] = _rms(x_ref[...], g_ref[0]).astype(o_ref.dtype)


def _rmsnorm(x, gain, out_dtype, tm=512):
    rows, d = x.shape
    tm = min(tm, rows)
    g, layer = gain
    return pl.pallas_call(
        _rmsnorm_kernel,
        grid=(rows // tm,),
        in_specs=[pl.BlockSpec((tm, d), lambda i: (i, 0)),
                  pl.BlockSpec((1, 1, d), lambda i: (layer, 0, 0))],
        out_specs=pl.BlockSpec((tm, d), lambda i: (i, 0)),
        out_shape=jax.ShapeDtypeStruct((rows, d), out_dtype),
        compiler_params=_params(1),
        name="rmsnorm",
    )(x, g)


def _mm_kernel(*refs, n_w, n_row, n_full, n_out, has_norm, n_lhs, lhs_fn, epilogue):
    x_ref = refs[0]
    refs = refs[1:]
    g_ref = refs[0] if has_norm else None
    refs = refs[1:] if has_norm else refs
    lhs_refs = refs[:n_lhs]
    refs = refs[n_lhs:]
    w_refs = refs[:n_w]
    row_refs = refs[n_w:n_w + n_row]
    full_refs = refs[n_w + n_row:n_w + n_row + n_full]
    o_refs = refs[n_w + n_row + n_full:n_w + n_row + n_full + n_out]
    wbf_refs = refs[n_out + n_w + n_row + n_full:]

    @pl.when(pl.program_id(1) == 0)
    def _():
        for w_ref, wbf_ref in zip(w_refs, wbf_refs):
            wbf_ref[...] = w_ref[0].astype(BF16)

    if has_norm:
        xb = _rms(x_ref[...], g_ref[0]).astype(BF16)
    elif lhs_fn is not None:
        xb = lhs_fn(x_ref[...], *[r[...] for r in lhs_refs]).astype(BF16)
    else:
        xb = x_ref[...].astype(BF16)
    accs = [jnp.dot(xb, wbf_ref[...], preferred_element_type=F32) for wbf_ref in wbf_refs]
    out = epilogue(accs, [r[0] for r in row_refs], [f[...] for f in full_refs])
    outs = out if isinstance(out, (list, tuple)) else [out]
    for o_ref, o in zip(o_refs, outs):
        o_ref[...] = o.astype(o_ref.dtype)


def _mm(x, ws, n_cols, epilogue, out_dtype, *, norm_g=None, lhs_fn=None, lhs_extra=(),
        rows_extra=(), full_extra=(), tm=1024, tn=512, single_w=False, n_out=1, name="mm"):
    rows, k = x.shape
    tm = min(tm, rows)
    tn = min(tn, n_cols)
    in_specs = [pl.BlockSpec((tm, k), lambda j, i: (i, 0))]
    args = [x]
    if norm_g is not None:
        g, g_layer = norm_g
        in_specs.append(pl.BlockSpec((1, 1, k), lambda j, i: (g_layer, 0, 0)))
        args.append(g)
    for e in lhs_extra:
        if e.shape[0] == 1:
            in_specs.append(pl.BlockSpec((1, k), lambda j, i: (0, 0)))
        else:
            in_specs.append(pl.BlockSpec((tm, k), lambda j, i: (i, 0)))
        args.append(e)
    w_mode = dict(pipeline_mode=pl.Buffered(1)) if single_w else {}
    for w, layer, off in ws:
        in_specs.append(pl.BlockSpec((1, k, tn), lambda j, i, l=layer, o=off // tn: (l, 0, o + j),
                                     **w_mode))
        args.append(w)
    for v, layer, off in rows_extra:
        in_specs.append(pl.BlockSpec((1, 1, tn), lambda j, i, l=layer, o=off // tn: (l, 0, o + j)))
        args.append(v)
    for f in full_extra:
        in_specs.append(pl.BlockSpec((tm, tn), lambda j, i: (i, j)))
        args.append(f)
    kern = functools.partial(_mm_kernel, n_w=len(ws), n_row=len(rows_extra),
                             n_full=len(full_extra), n_out=n_out, has_norm=norm_g is not None,
                             n_lhs=len(lhs_extra), lhs_fn=lhs_fn, epilogue=epilogue)
    out_spec = pl.BlockSpec((tm, tn), lambda j, i: (i, j))
    out_shape = jax.ShapeDtypeStruct((rows, n_cols), out_dtype)
    res = pl.pallas_call(
        kern,
        grid=(n_cols // tn, rows // tm),
        in_specs=in_specs,
        out_specs=[out_spec] * n_out,
        out_shape=[out_shape] * n_out,
        scratch_shapes=[pltpu.VMEM((k, tn), BF16) for _ in ws],
        compiler_params=_params(2),
        name=name,
    )(*args)
    return res[0] if n_out == 1 else res


def _ep_each(accs, rows, fulls):
    return accs


def _ep_plain(accs, rows, fulls):
    return accs[0]


def _ep_residual(accs, rows, fulls):
    return fulls[0] + accs[0]


def _ep_glu_residual(accs, rows, fulls):
    a = accs[0] + rows[0]
    b = accs[1] + rows[1]
    return fulls[0] + a * jax.nn.sigmoid(b)


def _mlp_kernel(x_ref, g_ref, wu_ref, wd_ref, o_ref, xn_ref):
    @pl.when(pl.program_id(1) == 0)
    def _():
        xv = x_ref[...]
        xn_ref[...] = _rms(xv, g_ref[0]).astype(BF16)
        o_ref[...] = xv

    h = jnp.dot(xn_ref[...], wu_ref[...], preferred_element_type=F32)
    h = jnp.maximum(h, 0.0)
    o_ref[...] += jnp.dot((h * h).astype(BF16), wd_ref[...], preferred_element_type=F32)


def _mlp(x, gain, wu_bf, wd_bf, tm=1024, tf=512):
    rows, d = x.shape
    dff = wu_bf.shape[1]
    tm = min(tm, rows)
    g, layer = gain
    return pl.pallas_call(
        _mlp_kernel,
        grid=(rows // tm, dff // tf),
        in_specs=[pl.BlockSpec((tm, d), lambda i, f: (i, 0)),
                  pl.BlockSpec((1, 1, d), lambda i, f: (layer, 0, 0)),
                  pl.BlockSpec((d, tf), lambda i, f: (0, f)),
                  pl.BlockSpec((tf, d), lambda i, f: (f, 0))],
        out_specs=pl.BlockSpec((tm, d), lambda i, f: (i, 0)),
        out_shape=jax.ShapeDtypeStruct((rows, d), F32),
        scratch_shapes=[pltpu.VMEM((tm, d), BF16)],
        compiler_params=_params(2),
        name="mlp",
    )(x, g, wu_bf, wd_bf)


def _mlp_cast_kernel(x_ref, g_ref, wu_ref, wd_ref, o_ref, wub_ref, wdb_ref, xn_ref):
    @pl.when(pl.program_id(0) == 0)
    def _():
        xv = x_ref[...]
        xn_ref[...] = _rms(xv, g_ref[0]).astype(BF16)
        o_ref[...] = xv

    wu = wu_ref[0].astype(BF16)
    wd = wd_ref[0].astype(BF16)
    wub_ref[...] = wu
    wdb_ref[...] = wd
    h = jnp.dot(xn_ref[...], wu, preferred_element_type=F32)
    h = jnp.maximum(h, 0.0)
    o_ref[...] += jnp.dot((h * h).astype(BF16), wd, preferred_element_type=F32)


def _mlp_cast(x, gain, w_up, w_down, layer, tf=512):
    rows, d = x.shape
    dff = w_up.shape[2]
    g, g_layer = gain
    return pl.pallas_call(
        _mlp_cast_kernel,
        grid=(dff // tf,),
        in_specs=[pl.BlockSpec((rows, d), lambda f: (0, 0)),
                  pl.BlockSpec((1, 1, d), lambda f: (g_layer, 0, 0)),
                  pl.BlockSpec((1, d, tf), lambda f: (layer, 0, f)),
                  pl.BlockSpec((1, tf, d), lambda f: (layer, f, 0))],
        out_specs=[pl.BlockSpec((rows, d), lambda f: (0, 0)),
                   pl.BlockSpec((d, tf), lambda f: (0, f)),
                   pl.BlockSpec((tf, d), lambda f: (f, 0))],
        out_shape=[jax.ShapeDtypeStruct((rows, d), F32),
                   jax.ShapeDtypeStruct((d, dff), BF16),
                   jax.ShapeDtypeStruct((dff, d), BF16)],
        scratch_shapes=[pltpu.VMEM((rows, d), BF16)],
        compiler_params=_params(1),
        name="mlp_cast",
    )(x, g, w_up, w_down)


def _attn_prompt_kernel(q_ref, k_ref, v_ref, o_ref, *, scale):
    q = q_ref[...]
    k = k_ref[0].astype(BF16)
    s = lax.dot_general(q, k, (((1,), (1,)), ((), ())), preferred_element_type=F32) * scale
    m = jnp.max(s, axis=-1, keepdims=True)
    e = jnp.exp(s - m)
    p = e / jnp.sum(e, axis=-1, keepdims=True)
    o = jnp.dot(p.astype(BF16), v_ref[0].astype(BF16), preferred_element_type=F32)
    o_ref[...] = o.astype(o_ref.dtype)


def _attn_prompt(q, k, v, n_seq, seq, heads, tq=1024):
    rows, d = q.shape
    m = k.shape[1]
    hd = d // heads
    tq = min(tq, seq)
    tps = seq // tq
    kern = functools.partial(_attn_prompt_kernel, scale=hd ** -0.5)
    return pl.pallas_call(
        kern,
        grid=(n_seq, tps, heads),
        in_specs=[pl.BlockSpec((tq, hd), lambda b, i, h: (b * tps + i, h)),
                  pl.BlockSpec((1, m, hd), lambda b, i, h: (b, 0, h)),
                  pl.BlockSpec((1, m, hd), lambda b, i, h: (b, 0, h))],
        out_specs=pl.BlockSpec((tq, hd), lambda b, i, h: (b * tps + i, h)),
        out_shape=jax.ShapeDtypeStruct((rows, d), BF16),
        compiler_params=_params(3),
        name="attn_prompt",
    )(q, k, v)


def _slab_view(x, heads):
    lead = x.shape[:-2]
    hd = x.shape[-1]
    n = len(lead)
    x = x.reshape(lead + (heads, hd // LANES, LANES))
    x = x.transpose(tuple(range(n)) + (n + 1, n, n + 2))
    return x.reshape(lead + (heads * hd // (SUBLANES * LANES), SUBLANES, LANES))


def _slab_unview(x, heads):
    lead = x.shape[:-3]
    n = len(lead)
    tiles = x.shape[-3] * SUBLANES // heads
    x = x.reshape(lead + (tiles, heads, LANES))
    x = x.transpose(tuple(range(n)) + (n + 1, n, n + 2))
    return x.reshape(lead + (heads, tiles * LANES))


def _attn_sample_kernel(q_ref, k_ref, v_ref, o_ref, *, heads, scale, bt):
    for b in range(bt):
        t = jnp.sum(k_ref[0, b] * q_ref[b][None], axis=1)
        t = t + pltpu.roll(t, heads, 1)
        s = jnp.sum(t, axis=-1, keepdims=True) * scale
        mx = jnp.max(s, axis=0, keepdims=True)
        e = jnp.exp(s - mx)
        p = e / jnp.sum(e, axis=0, keepdims=True)
        o_ref[b] = jnp.sum(p[:, None] * v_ref[0, b], axis=0)


def _attn_sample(q, cache_k, cache_v, layer, heads, bt=2):
    bsz, pairs = q.shape[:2]
    m = cache_k.shape[2]
    hd = pairs * SUBLANES * LANES // heads
    kern = functools.partial(_attn_sample_kernel, heads=heads, scale=hd ** -0.5, bt=bt)
    kv_spec = pl.BlockSpec((1, bt, m, pairs, SUBLANES, LANES), lambda i: (layer, i, 0, 0, 0, 0))
    q_spec = pl.BlockSpec((bt, pairs, SUBLANES, LANES), lambda i: (i, 0, 0, 0))
    return pl.pallas_call(
        kern,
        grid=(bsz // bt,),
        in_specs=[q_spec, kv_spec, kv_spec],
        out_specs=q_spec,
        out_shape=jax.ShapeDtypeStruct(q.shape, F32),
        compiler_params=_params(1),
        name="attn_sample",
    )(q, cache_k, cache_v)


def _lru_gates(xc, wa, ba, wi, bi, lam):
    xcb = xc.astype(BF16)
    r = jax.nn.sigmoid(jnp.dot(xcb, wa.astype(BF16), preferred_element_type=F32) + ba)
    ig = jax.nn.sigmoid(jnp.dot(xcb, wi.astype(BF16), preferred_element_type=F32) + bi)
    log_a = -LRU_C * r * jax.nn.softplus(-lam)
    a = jnp.exp(log_a)
    u = jnp.sqrt(-jnp.tanh(log_a) * (a * a + 1.0)) * (ig * xc)
    return a, u


def _lru_prompt_kernel(xb_ref, gt_ref, cw_ref, cb_ref, wa_ref, ba_ref, wi_ref, bi_ref, lam_ref,
                       y_ref, cst_ref, hl_ref, ext_ref, a_ref, u_ref, hs_ref, hc_ref,
                       *, tiles_per_seq, taps):
    tm, cb = xb_ref.shape
    halo = SUBLANES

    @pl.when(pl.program_id(1) % tiles_per_seq == 0)
    def _():
        ext_ref[0:halo, :] = jnp.zeros((halo, cb), F32)
        hc_ref[...] = jnp.zeros((1, cb), F32)

    ext_ref[halo:halo + tm, :] = xb_ref[...]
    cw = cw_ref[...]
    xc = cb_ref[...]
    for k in range(taps):
        start = halo - (taps - 1) + k
        xc = xc + cw[k:k + 1, :] * ext_ref[start:start + tm, :]
    cst_ref[0] = ext_ref[halo + tm - (taps - 1):halo + tm, :]
    ext_ref[0:halo, :] = ext_ref[tm:tm + halo, :]

    a, u = _lru_gates(xc, wa_ref[0], ba_ref[...], wi_ref[0], bi_ref[...], lam_ref[...])

    groups = tm // SUBLANES
    a = a.reshape(groups, SUBLANES, cb)
    u = u.reshape(groups, SUBLANES, cb)
    row = lax.broadcasted_iota(jnp.int32, (groups, SUBLANES, cb), 1)
    s = 1
    while s < SUBLANES:
        keep = row >= s
        u = u + a * jnp.where(keep, pltpu.roll(u, s, 1), 0.0)
        a = a * jnp.where(keep, pltpu.roll(a, s, 1), 1.0)
        s *= 2
    a_ref[...] = a.reshape(tm, cb)
    u_ref[...] = u.reshape(tm, cb)

    def link(g, h):
        off = pl.multiple_of(g * SUBLANES, SUBLANES)
        ag = a_ref[pl.ds(off, SUBLANES), :]
        ug = u_ref[pl.ds(off, SUBLANES), :]
        hs_ref[pl.ds(off, SUBLANES), :] = ag * h + ug
        return ag[SUBLANES - 1:SUBLANES, :] * h + ug[SUBLANES - 1:SUBLANES, :]

    h_last = lax.fori_loop(0, tm // SUBLANES, link, hc_ref[...], unroll=4)
    hc_ref[...] = h_last
    hl_ref[0] = h_last
    y_ref[...] = (hs_ref[...] * jax.nn.gelu(gt_ref[...])).astype(y_ref.dtype)


def _lru_prompt(proj, conv_w, conv_b, w_a, b_a, w_i, b_i, lam, n_seq, seq, tm=1024):
    rows, r2 = proj.shape
    r = r2 // 2
    nb, cb = w_a.shape[0], w_a.shape[1]
    taps = conv_w.shape[0]
    tm = min(tm, seq)
    tps = seq // tm
    kern = functools.partial(_lru_prompt_kernel, tiles_per_seq=tps, taps=taps)
    vec = lambda c, i: (0, c)
    y, cst, hl = pl.pallas_call(
        kern,
        grid=(nb, rows // tm),
        in_specs=[pl.BlockSpec((tm, cb), lambda c, i: (i, c)),
                  pl.BlockSpec((tm, cb), lambda c, i: (i, nb + c)),
                  pl.BlockSpec((taps, cb), vec),
                  pl.BlockSpec((1, cb), vec),
                  pl.BlockSpec((1, cb, cb), lambda c, i: (c, 0, 0)),
                  pl.BlockSpec((1, cb), vec),
                  pl.BlockSpec((1, cb, cb), lambda c, i: (c, 0, 0)),
                  pl.BlockSpec((1, cb), vec),
                  pl.BlockSpec((1, cb), vec)],
        out_specs=[pl.BlockSpec((tm, cb), lambda c, i: (i, c)),
                   pl.BlockSpec((1, taps - 1, cb), lambda c, i: (i // tps, 0, c)),
                   pl.BlockSpec((1, 1, cb), lambda c, i: (i // tps, 0, c))],
        out_shape=[jax.ShapeDtypeStruct((rows, r), BF16),
                   jax.ShapeDtypeStruct((n_seq, taps - 1, r), F32),
                   jax.ShapeDtypeStruct((n_seq, 1, r), F32)],
        scratch_shapes=[pltpu.VMEM((tm + SUBLANES, cb), F32),
                        pltpu.VMEM((tm, cb), F32),
                        pltpu.VMEM((tm, cb), F32),
                        pltpu.VMEM((tm, cb), F32),
                        pltpu.VMEM((1, cb), F32)],
        compiler_params=_params(2),
        name="lru_prompt",
    )(proj, proj, conv_w, conv_b.reshape(1, r), w_a, b_a.reshape(1, r), w_i, b_i.reshape(1, r),
      lam.reshape(1, r))
    return y, cst, hl.reshape(n_seq, r)


def _lru_sample_kernel(xb_ref, gt_ref, c0_ref, c1_ref, c2_ref, h0_ref, cw_ref, cb_ref,
                       wa_ref, ba_ref, wi_ref, bi_ref, lam_ref, y_ref, h_ref):
    cw = cw_ref[...]
    xc = cb_ref[...]
    for k, c_ref in enumerate((c0_ref, c1_ref, c2_ref, xb_ref)):
        xc = xc + cw[k:k + 1, :] * c_ref[...]
    a, u = _lru_gates(xc, wa_ref[0], ba_ref[...], wi_ref[0], bi_ref[...], lam_ref[...])
    h = a * h0_ref[...] + u
    h_ref[...] = h
    y_ref[...] = (h * jax.nn.gelu(gt_ref[...])).astype(y_ref.dtype)


def _lru_sample(proj, conv_rows, h0, conv_w, conv_b, w_a, b_a, w_i, b_i, lam):
    bsz, r2 = proj.shape
    r = r2 // 2
    nb, cb = w_a.shape[0], w_a.shape[1]
    taps = conv_w.shape[0]
    blk = lambda c: (0, c)
    return pl.pallas_call(
        _lru_sample_kernel,
        grid=(nb,),
        in_specs=[pl.BlockSpec((bsz, cb), blk),
                  pl.BlockSpec((bsz, cb), lambda c: (0, nb + c)),
                  pl.BlockSpec((bsz, cb), blk), pl.BlockSpec((bsz, cb), blk),
                  pl.BlockSpec((bsz, cb), blk), pl.BlockSpec((bsz, cb), blk),
                  pl.BlockSpec((taps, cb), blk), pl.BlockSpec((1, cb), blk),
                  pl.BlockSpec((1, cb, cb), lambda c: (c, 0, 0)), pl.BlockSpec((1, cb), blk),
                  pl.BlockSpec((1, cb, cb), lambda c: (c, 0, 0)), pl.BlockSpec((1, cb), blk),
                  pl.BlockSpec((1, cb), blk)],
        out_specs=[pl.BlockSpec((bsz, cb), blk), pl.BlockSpec((bsz, cb), blk)],
        out_shape=[jax.ShapeDtypeStruct((bsz, r), BF16), jax.ShapeDtypeStruct((bsz, r), F32)],
        compiler_params=_params(1),
        name="lru_sample",
    )(proj, proj, conv_rows[0], conv_rows[1], conv_rows[2], h0, conv_w, conv_b.reshape(1, r),
      w_a, b_a.reshape(1, r), w_i, b_i.reshape(1, r), lam.reshape(1, r))


def _pool_prompt_kernel(x_ref, g_ref, w_ref, b_ref, sc_ref, o_ref, st_ref, ext_ref, wbf_ref,
                        *, tiles_per_seq, windows):
    tm, d = x_ref.shape
    gd = d // len(windows)
    halo = 2 * SUBLANES
    i = pl.program_id(0)

    @pl.when(i == 0)
    def _():
        wbf_ref[...] = w_ref[...].astype(BF16)

    @pl.when(i % tiles_per_seq == 0)
    def _():
        ext_ref[0:halo, :] = jnp.zeros((halo, d), F32)

    ext_ref[halo:halo + tm, :] = _rms(x_ref[...], g_ref[0])
    pos = (i % tiles_per_seq) * tm + lax.broadcasted_iota(jnp.int32, (tm, 1), 0)
    for g, w in enumerate(windows):
        sl = slice(g * gd, (g + 1) * gd)
        s = ext_ref[:, sl]
        shift = 1
        while shift < w:
            s = s + pltpu.roll(s, shift, 0)
            shift *= 2
        cnt = jnp.minimum(pos + 1, w).astype(F32)
        pooled = s[halo:, :] / cnt
        diff = (pooled - ext_ref[halo:halo + tm, sl]).astype(BF16)
        mixed = jnp.dot(diff, wbf_ref[g], preferred_element_type=F32) + b_ref[:, sl]
        o_ref[:, sl] = x_ref[:, sl] + mixed * sc_ref[:, sl]
    st_ref[0] = ext_ref[halo + tm - POOL_BUF:halo + tm, :]
    ext_ref[0:halo, :] = ext_ref[tm:tm + halo, :]


def _pool_prompt(x, gain, w, b, scale, n_seq, seq, tm=512):
    rows, d = x.shape
    tm = min(tm, seq)
    tps = seq // tm
    ng, gd = w.shape[0], w.shape[1]
    g, layer = gain
    kern = functools.partial(_pool_prompt_kernel, tiles_per_seq=tps, windows=POOL_WINDOWS)
    return pl.pallas_call(
        kern,
        grid=(rows // tm,),
        in_specs=[pl.BlockSpec((tm, d), lambda i: (i, 0)),
                  pl.BlockSpec((1, 1, d), lambda i: (layer, 0, 0)),
                  pl.BlockSpec((ng, gd, gd), lambda i: (0, 0, 0)),
                  pl.BlockSpec((1, d), lambda i: (0, 0)),
                  pl.BlockSpec((1, d), lambda i: (0, 0))],
        out_specs=[pl.BlockSpec((tm, d), lambda i: (i, 0)),
                   pl.BlockSpec((1, POOL_BUF, d), lambda i: (i // tps, 0, 0))],
        out_shape=[jax.ShapeDtypeStruct((rows, d), F32),
                   jax.ShapeDtypeStruct((n_seq, POOL_BUF, d), F32)],
        scratch_shapes=[pltpu.VMEM((tm + 2 * SUBLANES, d), F32),
                        pltpu.VMEM((ng, gd, gd), BF16)],
        compiler_params=_params(1),
        name="pool_prompt",
    )(x, g, w, b.reshape(1, d), scale.reshape(1, d))


def _pool_sample_kernel(xn_ref, x_ref, st_ref, w_ref, b_ref, sc_ref, o_ref, *, window, pos0):
    xn = xn_ref[...]
    s = xn
    for r in range(POOL_BUF - (window - 1), POOL_BUF):
        s = s + st_ref[r]
    pooled = s / float(min(pos0 + 1, window))
    diff = (pooled - xn).astype(BF16)
    mixed = jnp.dot(diff, w_ref[0].astype(BF16), preferred_element_type=F32) + b_ref[...]
    o_ref[...] = x_ref[...] + mixed * sc_ref[...]


def _pool_sample(xn, x, state_t, w, b, scale, pos0):
    bsz, d = xn.shape
    ng, gd = w.shape[0], w.shape[1]
    outs = []
    for g, window in enumerate(POOL_WINDOWS):
        kern = functools.partial(_pool_sample_kernel, window=window, pos0=pos0)
        blk = lambda i, g=g: (0, g)
        outs.append(pl.pallas_call(
            kern,
            grid=(1,),
            in_specs=[pl.BlockSpec((bsz, gd), blk), pl.BlockSpec((bsz, gd), blk),
                      pl.BlockSpec((POOL_BUF, bsz, gd), lambda i, g=g: (0, 0, g)),
                      pl.BlockSpec((1, gd, gd), lambda i, g=g: (g, 0, 0)),
                      pl.BlockSpec((1, gd), blk), pl.BlockSpec((1, gd), blk)],
            out_specs=pl.BlockSpec((bsz, gd), lambda i: (0, 0)),
            out_shape=jax.ShapeDtypeStruct((bsz, gd), F32),
            compiler_params=_params(1),
            name="pool_sample",
        )(xn, x, state_t, w, b.reshape(1, d), scale.reshape(1, d)))
    return jnp.concatenate(outs, axis=-1)


def _s5_matrices(a_re, a_im, b_re, b_im, c_re, c_im, log_dt, chunk):
    g_all, p = a_re.shape
    c = b_re.shape[-1]
    gpb = S5_GROUPS_PER_BLOCK
    nb = g_all // gpb
    dt = jnp.exp(log_dt.astype(F32))[:, None]
    mag = jnp.exp(a_re * dt)
    lr, li = mag * jnp.cos(a_im * dt), mag * jnp.sin(a_im * dt)
    den = a_re * a_re + a_im * a_im
    q_re = ((lr - 1.0) * a_re + li * a_im) / den
    q_im = (li * a_re - (lr - 1.0) * a_im) / den
    bb_re = q_re[..., None] * b_re - q_im[..., None] * b_im
    bb_im = q_re[..., None] * b_im + q_im[..., None] * b_re

    def cmul(xr, xi, yr, yi):
        return xr * yr - xi * yi, xr * yi + xi * yr

    pw = [(jnp.ones_like(lr), jnp.zeros_like(lr))]
    for _ in range(chunk):
        pw.append(cmul(pw[-1][0], pw[-1][1], lr, li))
    hi = lax.Precision.HIGHEST
    lane_g = jnp.arange(LANES) // c
    state_g = jnp.arange(gpb * p) // p
    mask_cc = (lane_g[:, None] == lane_g[None, :]).astype(F32)
    mask_cp = (lane_g[:, None] == state_g[None, :]).astype(F32)
    spread = jnp.tile(jnp.eye(p, dtype=F32), (1, gpb))

    def lanes_by_state(x):
        return x.transpose(0, 1, 3, 2).reshape(x.shape[0], nb, LANES, p)

    def state_by_lanes(x):
        return x.transpose(0, 3, 1, 2).reshape(x.shape[0], p, nb, LANES).transpose(0, 2, 1, 3)

    pr_all = jnp.stack([x[0] for x in pw])
    pi_all = jnp.stack([x[1] for x in pw])
    wr, wi = cmul(c_re[None], c_im[None], pr_all[:, :, None, :], pi_all[:, :, None, :])
    wr, wi = state_by_lanes(wr), state_by_lanes(wi)

    bb_rows = jnp.concatenate([lanes_by_state(bb_re[None])[0], -lanes_by_state(bb_im[None])[0]],
                              axis=-1)
    w_cols = jnp.concatenate([wr[:chunk], wi[:chunk]], axis=2)
    blocks = mask_cc * jnp.einsum('brk,tbkc->tbrc', bb_rows, w_cols, precision=hi)
    zero = jnp.zeros_like(blocks[0])
    m_in = jnp.stack([jnp.concatenate([blocks[t - s] if t >= s else zero for t in range(chunk)],
                                      axis=-1) for s in range(chunk)], axis=1)
    m_in = m_in.reshape(nb, chunk * LANES, chunk * LANES).astype(BF16)

    pr_rev, pi_rev = pr_all[:chunk][::-1], pi_all[:chunk][::-1]
    er, ei = cmul(pr_rev[..., None], pi_rev[..., None], bb_re[None], bb_im[None])
    halves = [mask_cp * jnp.einsum('tbrp,pq->tbrq', lanes_by_state(e), spread, precision=hi)
              for e in (er, ei)]
    m_state = jnp.concatenate(halves, axis=-1).transpose(1, 0, 2, 3)
    m_state = m_state.reshape(nb, chunk * LANES, 2 * gpb * p).astype(BF16)

    halves = [mask_cp.T * jnp.einsum('qp,tbpc->tbqc', spread.T, w, precision=hi)
              for w in (wr[1:], -wi[1:])]
    m_out = jnp.concatenate(halves, axis=2).transpose(1, 2, 0, 3)
    m_out = m_out.reshape(nb, 2 * gpb * p, chunk * LANES).astype(BF16)

    qr, qi = pw[chunk]
    rows = [(jnp.ones_like(qr), jnp.zeros_like(qr))]
    for _ in range(SUBLANES):
        rows.append(cmul(rows[-1][0], rows[-1][1], qr, qi))
    table = jnp.stack([_s5_state_layout(r_, i_) for r_, i_ in rows])
    table = jnp.concatenate([table, jnp.zeros((2 * SUBLANES - table.shape[0], table.shape[1]), F32)])
    return m_in, m_state, m_out, table


def _s5_state_layout(re, im):
    lead = re.shape[:-2]
    g_all, p = re.shape[-2:]
    nb = g_all // S5_GROUPS_PER_BLOCK
    st = jnp.stack([re.reshape(lead + (nb, S5_GROUPS_PER_BLOCK * p)),
                    im.reshape(lead + (nb, S5_GROUPS_PER_BLOCK * p))], axis=-2)
    return st.reshape(lead + (2 * g_all * p,))


def _s5_state_unlayout(flat, g_all, p):
    lead = flat.shape[:-1]
    nb = g_all // S5_GROUPS_PER_BLOCK
    st = flat.reshape(lead + (nb, 2, S5_GROUPS_PER_BLOCK * p))
    return st[..., 0, :].reshape(lead + (g_all, p)), st[..., 1, :].reshape(lead + (g_all, p))


def _s5_chunk_rows(t, tr, chunk):
    return pl.ds(t, tr, stride=chunk) if chunk > 1 else pl.ds(0, tr)


def _s5_lhs(u_ref, chunk):
    tr = u_ref.shape[0] // chunk
    parts = [u_ref[_s5_chunk_rows(t, tr, chunk), :] for t in range(chunk)]
    return jnp.concatenate(parts, axis=1).astype(BF16)


def _s5_state_in_kernel(u_ref, m_ref, e_ref, *, chunk):
    e_ref[...] = jnp.dot(_s5_lhs(u_ref, chunk), m_ref[0], preferred_element_type=F32)


def _s5_state_in(xn, m_state, chunk, tr=1024):
    rows, d = xn.shape
    r = rows // chunk
    nb, _, sw = m_state.shape
    tr = min(tr, r)
    kern = functools.partial(_s5_state_in_kernel, chunk=chunk)
    return pl.pallas_call(
        kern,
        grid=(nb, r // tr),
        in_specs=[pl.BlockSpec((tr * chunk, LANES), lambda b, i: (i, b)),
                  pl.BlockSpec((1, chunk * LANES, sw), lambda b, i: (b, 0, 0))],
        out_specs=pl.BlockSpec((tr, sw), lambda b, i: (i, b)),
        out_shape=jax.ShapeDtypeStruct((r, nb * sw), F32),
        compiler_params=_params(2),
        name="s5_state_in",
    )(xn, m_state)


def _s5_scan_kernel(e_ref, tab_ref, h_ref, fin_ref):
    rps, sw = e_ref.shape
    half = sw // 2
    groups = rps // SUBLANES
    er = e_ref[:, :half].reshape(groups, SUBLANES, half)
    ei = e_ref[:, half:].reshape(groups, SUBLANES, half)
    row = lax.broadcasted_iota(jnp.int32, (groups, SUBLANES, half), 1)
    s = 1
    while s < SUBLANES:
        keep = row >= s
        pr, pi = tab_ref[s:s + 1, :half], tab_ref[s:s + 1, half:]
        sr = jnp.where(keep, pltpu.roll(er, s, 1), 0.0)
        si = jnp.where(keep, pltpu.roll(ei, s, 1), 0.0)
        er, ei = er + pr * sr - pi * si, ei + pr * si + pi * sr
        s *= 2
    xr = jnp.where(row >= 1, pltpu.roll(er, 1, 1), 0.0)
    xi = jnp.where(row >= 1, pltpu.roll(ei, 1, 1), 0.0)
    ar, ai = tab_ref[0:SUBLANES, :half], tab_ref[0:SUBLANES, half:]
    l8r, l8i = tab_ref[SUBLANES:SUBLANES + 1, :half], tab_ref[SUBLANES:SUBLANES + 1, half:]
    cr = jnp.zeros((1, half), F32)
    ci = jnp.zeros((1, half), F32)
    for g in range(groups):
        lo, hi = g * SUBLANES, (g + 1) * SUBLANES
        h_ref[lo:hi, :half] = ar * cr - ai * ci + xr[g]
        h_ref[lo:hi, half:] = ar * ci + ai * cr + xi[g]
        cr, ci = (l8r * cr - l8i * ci + er[g, SUBLANES - 1:SUBLANES],
                  l8r * ci + l8i * cr + ei[g, SUBLANES - 1:SUBLANES])
    fin_ref[0, :, :half] = cr
    fin_ref[0, :, half:] = ci


def _s5_scan(e, table, n_seq, sw):
    r, width = e.shape
    rps = r // n_seq
    return pl.pallas_call(
        _s5_scan_kernel,
        grid=(width // sw, n_seq),
        in_specs=[pl.BlockSpec((rps, sw), lambda c, b: (b, c)),
                  pl.BlockSpec((2 * SUBLANES, sw), lambda c, b: (0, c))],
        out_specs=[pl.BlockSpec((rps, sw), lambda c, b: (b, c)),
                   pl.BlockSpec((1, 1, sw), lambda c, b: (b, 0, c))],
        out_shape=[jax.ShapeDtypeStruct((r, width), F32),
                   jax.ShapeDtypeStruct((n_seq, 1, width), F32)],
        compiler_params=_params(2),
        name="s5_scan",
    )(e, table)


def _s5_step_kernel(e_ref, h0_ref, tab_ref, o_ref):
    half = e_ref.shape[1] // 2
    lr, li = tab_ref[1:2, :half], tab_ref[1:2, half:]
    hr, hi = h0_ref[:, :half], h0_ref[:, half:]
    o_ref[:, :half] = lr * hr - li * hi + e_ref[:, :half]
    o_ref[:, half:] = lr * hi + li * hr + e_ref[:, half:]


def _s5_step(e, h0, table, sw):
    bsz, width = e.shape
    return pl.pallas_call(
        _s5_step_kernel,
        grid=(width // sw,),
        in_specs=[pl.BlockSpec((bsz, sw), lambda c: (0, c)),
                  pl.BlockSpec((bsz, sw), lambda c: (0, c)),
                  pl.BlockSpec((2 * SUBLANES, sw), lambda c: (0, c))],
        out_specs=pl.BlockSpec((bsz, sw), lambda c: (0, c)),
        out_shape=jax.ShapeDtypeStruct((bsz, width), F32),
        compiler_params=_params(1),
        name="s5_step",
    )(e, h0, table)


def _s5_out_kernel(u_ref, h_ref, mi_ref, mo_ref, y_ref, *, chunk):
    tr = h_ref.shape[0]
    y = jnp.dot(_s5_lhs(u_ref, chunk), mi_ref[0], preferred_element_type=F32)
    y = y + jnp.dot(h_ref[...].astype(BF16), mo_ref[0], preferred_element_type=F32)
    for t in range(chunk):
        y_ref[_s5_chunk_rows(t, tr, chunk), :] = y[:, t * LANES:(t + 1) * LANES]


def _s5_out(xn, h, m_in, m_out, chunk, tr=1024):
    rows, d = xn.shape
    r = rows // chunk
    nb, sw, _ = m_out.shape
    tr = min(tr, r)
    kern = functools.partial(_s5_out_kernel, chunk=chunk)
    return pl.pallas_call(
        kern,
        grid=(nb, r // tr),
        in_specs=[pl.BlockSpec((tr * chunk, LANES), lambda b, i: (i, b)),
                  pl.BlockSpec((tr, sw), lambda b, i: (i, b)),
                  pl.BlockSpec((1, chunk * LANES, chunk * LANES), lambda b, i: (b, 0, 0)),
                  pl.BlockSpec((1, sw, chunk * LANES), lambda b, i: (b, 0, 0))],
        out_specs=pl.BlockSpec((tr * chunk, LANES), lambda b, i: (i, b)),
        out_shape=jax.ShapeDtypeStruct((rows, d), F32),
        compiler_params=_params(2),
        name="s5_out",
    )(xn, h, m_in, m_out)


def _s5_glu_lhs(y, xn, dvec):
    return jax.nn.gelu(y + dvec * xn)


def _s5_mixer(x, xn, h0_flat, n_seq, prm, w_glu, b_glu, layer, chunk):
    a_re, a_im, b_re, b_im, c_re, c_im, dvec, log_dt = prm
    rows, d = xn.shape
    m_in, m_state, m_out, table = _s5_matrices(a_re, a_im, b_re, b_im, c_re, c_im, log_dt, chunk)
    sw = m_state.shape[-1]
    e = _s5_state_in(xn, m_state, chunk)
    if h0_flat is None:
        h, fin = _s5_scan(e, table, n_seq, sw)
        fin = fin.reshape(n_seq, -1)
    else:
        h = h0_flat
        fin = _s5_step(e, h0_flat, table, sw)
    y = _s5_out(xn, h, m_in, m_out, chunk)
    b_glu3 = b_glu.reshape(b_glu.shape[0], 1, 2 * d)
    tiles = dict(tm=512, tn=1024, single_w=True) if h0_flat is None else {}
    x_new = _mm(y, [(w_glu, layer, 0), (w_glu, layer, d)], d, _ep_glu_residual, F32,
                lhs_fn=_s5_glu_lhs, lhs_extra=[xn, dvec.reshape(1, d)],
                rows_extra=[(b_glu3, layer, 0), (b_glu3, layer, d)],
                full_extra=[x], name="s5_glu", **tiles)
    return x_new, fin


def kernel(x_prompt, x_sample, cache_mem_k, cache_mem_v, state_lru_conv, state_lru_h, state_pool, state_s5_re, state_s5_im, mem_prompt, g_mix, g_xattn, g_mem, g_mlp, g_final, w_q, w_k, w_v, w_o, w_up, w_down, lru_w_in, lru_conv_w, lru_conv_b, lru_w_a, lru_b_a, lru_w_i, lru_b_i, lru_lambda, lru_w_o, pool_w, pool_b, pool_scale, s5_a_re, s5_a_im, s5_b_re, s5_b_im, s5_c_re, s5_c_im, s5_d, s5_log_dt, s5_w_glu, s5_b_glu):
    n_seq, seq, d = x_prompt.shape
    bsz, dec_seq, _ = x_sample.shape
    assert dec_seq == 1, "the sample group advances one token per request"
    depth = g_mix.shape[0]
    mem_tokens = mem_prompt.shape[1]
    heads, head_dim = cache_mem_k.shape[3], cache_mem_k.shape[4]
    g_all, p_state = s5_a_re.shape[1], s5_a_re.shape[2]
    past_len = 16384

    xp = x_prompt.reshape(n_seq * seq, d)
    xs = x_sample.reshape(bsz, d)
    mem = mem_prompt.reshape(n_seq * mem_tokens, d)
    wide = dict(tm=512, tn=d, single_w=True)
    assert SUBLANES == 2 * heads and head_dim % (2 * LANES) == 0, "slab view needs 4 heads"
    cache_k_view = _slab_view(cache_mem_k, heads)
    cache_v_view = _slab_view(cache_mem_v, heads)
    g_mix3, g_xattn3, g_mem3, g_mlp3 = (g.reshape(depth, 1, d)
                                        for g in (g_mix, g_xattn, g_mem, g_mlp))

    mem_k, mem_v = [], []
    lru_conv_p, lru_h_p, lru_conv_s, lru_h_s = [], [], [], []
    pool_p, pool_s = [], []
    s5_re_p, s5_im_p, s5_re_s, s5_im_s = [], [], [], []

    for i in range(depth):
        kind, j = i % 3, i // 3
        if kind == 0:
            prm = (lru_conv_w[j], lru_conv_b[j], lru_w_a[j], lru_b_a[j], lru_w_i[j], lru_b_i[j],
                   lru_lambda[j])
            proj = _mm(xp, [(lru_w_in, j, 0)], 2 * d, _ep_plain, F32, norm_g=(g_mix3, i),
                       name="lru_in", **wide)
            y, cst, hl = _lru_prompt(proj, *prm, n_seq, seq)
            xp = _mm(y, [(lru_w_o, j, 0)], d, _ep_residual, F32, full_extra=[xp], name="lru_out",
                     **wide)
            lru_conv_p.append(cst)
            lru_h_p.append(hl)

            proj = _mm(xs, [(lru_w_in, j, 0)], 2 * d, _ep_plain, F32, norm_g=(g_mix3, i),
                       name="lru_in_s")
            conv_rows = [state_lru_conv[j][:, r, :] for r in range(state_lru_conv.shape[2])]
            y, h_new = _lru_sample(proj, conv_rows, state_lru_h[j], *prm)
            xs = _mm(y, [(lru_w_o, j, 0)], d, _ep_residual, F32, full_extra=[xs], name="lru_out_s")
            lru_conv_s.append(jnp.concatenate(
                [state_lru_conv[j][:, 1:, :], proj[:, None, :d]], axis=1))
            lru_h_s.append(h_new)
        elif kind == 1:
            xp, st = _pool_prompt(xp, (g_mix3, i), pool_w[j], pool_b[j], pool_scale[j], n_seq, seq)
            pool_p.append(st)

            xn = _rmsnorm(xs, (g_mix3, i), F32)
            xs = _pool_sample(xn, xs, jnp.swapaxes(state_pool[j], 0, 1), pool_w[j], pool_b[j],
                              pool_scale[j], past_len)
            pool_s.append(jnp.concatenate([state_pool[j][:, 1:, :], xn[:, None, :]], axis=1))
        else:
            prm = (s5_a_re[j], s5_a_im[j], s5_b_re[j], s5_b_im[j], s5_c_re[j], s5_c_im[j],
                   s5_d[j], s5_log_dt[j])
            xp, fin = _s5_mixer(xp, _rmsnorm(xp, (g_mix3, i), F32), None, n_seq, prm,
                                s5_w_glu, s5_b_glu, j, S5_CHUNK)
            re, im = _s5_state_unlayout(fin, g_all, p_state)
            s5_re_p.append(re)
            s5_im_p.append(im)

            h0 = _s5_state_layout(state_s5_re[j], state_s5_im[j])
            xs, fin = _s5_mixer(xs, _rmsnorm(xs, (g_mix3, i), F32), h0, bsz, prm,
                                s5_w_glu, s5_b_glu, j, 1)
            re, im = _s5_state_unlayout(fin, g_all, p_state)
            s5_re_s.append(re)
            s5_im_s.append(im)

        k, v = _mm(mem, [(w_k, i, 0), (w_v, i, 0)], d, _ep_each, F32, norm_g=(g_mem3, i),
                   n_out=2, name="mem_kv")
        mem_k.append(k.reshape(n_seq, mem_tokens, heads, head_dim))
        mem_v.append(v.reshape(n_seq, mem_tokens, heads, head_dim))
        q = _mm(xp, [(w_q, i, 0)], d, _ep_plain, BF16, norm_g=(g_xattn3, i), name="q", **wide)
        o = _attn_prompt(q, k.reshape(n_seq, mem_tokens, d), v.reshape(n_seq, mem_tokens, d),
                         n_seq, seq, heads)
        xp = _mm(o, [(w_o, i, 0)], d, _ep_residual, F32, full_extra=[xp], name="attn_out", **wide)

        q = _mm(xs, [(w_q, i, 0)], d, _ep_plain, F32, norm_g=(g_xattn3, i), name="q_s")
        o = _attn_sample(_slab_view(q.reshape(bsz, heads, head_dim), heads), cache_k_view,
                         cache_v_view, i, heads)
        xs = _mm(_slab_unview(o, heads).reshape(bsz, d), [(w_o, i, 0)], d, _ep_residual, F32,
                 full_extra=[xs], name="attn_out_s")

        xs, wu_bf, wd_bf = _mlp_cast(xs, (g_mlp3, i), w_up, w_down, i)
        xp = _mlp(xp, (g_mlp3, i), wu_bf, wd_bf)

    g_final3 = g_final.reshape(1, 1, d)
    y_prompt = _rmsnorm(xp, (g_final3, 0), F32).reshape(n_seq, seq, d)
    y_sample = _rmsnorm(xs, (g_final3, 0), F32).reshape(bsz, 1, d)
    return (y_prompt, y_sample,
            jnp.stack(mem_k), jnp.stack(mem_v),
            jnp.stack(lru_conv_p), jnp.stack(lru_h_p), jnp.stack(pool_p),
            jnp.stack(s5_re_p), jnp.stack(s5_im_p),
            jnp.stack(lru_conv_s), jnp.stack(lru_h_s), jnp.stack(pool_s),
            jnp.stack(s5_re_s), jnp.stack(s5_im_s))
```
